```python
import math
import jax, jax.numpy as jnp
from jax import lax
import numpy as np

D_MODEL = 1024
BATCH = 1
SEQ = 16384
DEPTH = 2
DEC_BATCH = 32
DEC_SEQ = 1
PAST_LEN = 16384
PAGE_SIZE = 128

D_FF = 2816
HEAD_DIM = 64
H_A = 4
H_B = 4
DK_B = 64
DV_B = 128
RET_CHUNK = 128
H_C = 8
MOBA_BLOCK = 256
MOBA_TOPK = 3
N_BUCKETS = 32
MAX_DISTANCE = 128
Q_BLOCK = 128
NORM_EPS = 1e-6
NEG = -1e30

W_A = H_A * 2 * HEAD_DIM
W_BQK = H_B * DK_B
W_BV = H_B * DV_B
W_C = H_C * HEAD_DIM
SPLITS = [W_A, W_A, W_A, W_BQK, W_BQK, W_BV, W_BV, W_C, W_C, W_C, 3 * D_MODEL]
IN_WIDTH = sum(SPLITS)

kernel_name = "hybrid_diff_retention_moba_decoder_step"


def rms_norm(x, g):
    xf = x.astype(jnp.float32)
    y = xf * lax.rsqrt(jnp.mean(xf * xf, axis=-1, keepdims=True) + NORM_EPS)
    return (y * g.astype(jnp.float32)).astype(x.dtype)


def rms_norm_plain(x):
    xf = x.astype(jnp.float32)
    return (xf * lax.rsqrt(jnp.mean(xf * xf, axis=-1, keepdims=True) + NORM_EPS)).astype(x.dtype)


def swiglu(h, w_gate, w_up, w_down):
    return (jax.nn.silu(h @ w_gate) * (h @ w_up)) @ w_down


def t5_bucket(rel):
    n = jnp.maximum(rel, 0)
    max_exact = N_BUCKETS // 2
    nf = jnp.maximum(n, 1).astype(jnp.float32)
    large = max_exact + (jnp.log(nf / max_exact) / math.log(MAX_DISTANCE / max_exact)
                         * (N_BUCKETS - max_exact)).astype(jnp.int32)
    large = jnp.minimum(large, N_BUCKETS - 1)
    return jnp.where(n < max_exact, n, large)


def rotary(x, pos):
    half = x.shape[-1] // 2
    inv = 1.0 / (10000.0 ** (jnp.arange(half, dtype=jnp.float32) / half))
    ang = pos.astype(jnp.float32)[:, None] * inv[None, :]
    cos = jnp.cos(ang)[:, None, :]
    sin = jnp.sin(ang)[:, None, :]
    xf = x.astype(jnp.float32)
    x1, x2 = xf[..., :half], xf[..., half:]
    return jnp.concatenate([x1 * cos - x2 * sin, x1 * sin + x2 * cos], axis=-1).astype(x.dtype)


def map_query_blocks(fn, q, qpos, qb):
    B, Q = q.shape[:2]
    nb = Q // qb
    qs = jnp.moveaxis(q.reshape((B, nb, qb) + q.shape[2:]), 1, 0)
    ps = qpos.reshape(nb, qb)
    out = lax.map(lambda a: fn(a[0], a[1]), (qs, ps))
    return jnp.moveaxis(out, 0, 1).reshape((B, Q) + out.shape[3:])


def diff_attend(q, qpos, k, v, lam, bias_t):
    kpos = jnp.arange(k.shape[1], dtype=jnp.int32)
    logits = jnp.einsum('bqhcd,bkhcd->bchqk', q, k).astype(jnp.float32) * (HEAD_DIM ** -0.5)
    rel = qpos[:, None] - kpos[None, :]
    logits = jnp.where(rel >= 0, logits + bias_t[:, t5_bucket(rel)].astype(jnp.float32), NEG)
    p = jax.nn.softmax(logits, axis=-1)
    a = p[:, 0] - lam * p[:, 1]
    return jnp.einsum('bhqk,bkhe->bqhe', a.astype(v.dtype), v)


def moba_attend(q, qpos, kblk, vblk, k_mean, bias_t):
    B, Q, H, hd = q.shape
    n_blk = kblk.shape[2]
    own = qpos // MOBA_BLOCK
    gate = jnp.einsum('bqhd,bhnd->bhqn', q.astype(jnp.float32), k_mean)
    past = jnp.arange(n_blk)[None, :] < own[:, None]
    gate = jnp.where(past, gate, -jnp.inf)
    _, top = lax.top_k(gate, min(MOBA_TOPK, n_blk))
    idx = jnp.concatenate([top, jnp.broadcast_to(own[None, None, :, None], (B, H, Q, 1))], axis=-1)
    sel_ok = jnp.concatenate([top < own[:, None], jnp.ones((B, H, Q, 1), dtype=bool)], axis=-1)
    bi = jnp.arange(B)[:, None, None, None]
    hi = jnp.arange(H)[None, :, None, None]
    ks = kblk[bi, hi, idx]
    vs = vblk[bi, hi, idx]
    kpos = idx[..., None] * MOBA_BLOCK + jnp.arange(MOBA_BLOCK, dtype=jnp.int32)
    rel = qpos[:, None, None] - kpos
    mask = sel_ok[..., None] & (rel >= 0)
    logits = jnp.einsum('bqhd,bhqjkd->bhqjk', q, ks).astype(jnp.float32) * (hd ** -0.5)
    logits = jnp.where(mask, logits + bias_t[hi[..., None], t5_bucket(rel)].astype(jnp.float32), NEG)
    p = jax.nn.softmax(logits.reshape(B, H, Q, -1), axis=-1).reshape(logits.shape)
    return jnp.einsum('bhqjk,bhqjkd->bqhd', p.astype(vs.dtype), vs)


def retention(q, k, v, s0, chunk):
    B, T, H, dk = q.shape
    dv = v.shape[-1]
    nc = T // chunk
    log_g = jnp.log(1.0 - 2.0 ** (-5.0 - jnp.arange(H, dtype=jnp.float32)))
    i = jnp.arange(chunk, dtype=jnp.float32)
    d = i[:, None] - i[None, :]
    decay = jnp.where(d >= 0, jnp.exp(jnp.maximum(d, 0.0)[None] * log_g[:, None, None]), 0.0)
    qc = q.reshape(B, nc, chunk, H, dk)
    kc = k.reshape(B, nc, chunk, H, dk)
    vc = v.reshape(B, nc, chunk, H, dv)
    scores = jnp.einsum('bclhd,bcmhd->bchlm', qc, kc) * decay
    o_inner = jnp.einsum('bchlm,bcmhe->bclhe', scores, vc)
    k_w = jnp.exp((chunk - 1 - i)[:, None] * log_g[None, :])
    kv = jnp.einsum('bcmhd,bcmhe->cbhde', kc * k_w[:, :, None], vc).astype(jnp.float32)
    g_c = jnp.exp(chunk * log_g)[None, :, None, None]

    def step(s, kv_c):
        return g_c * s + kv_c, s

    s_fin, s_prev = lax.scan(step, s0.astype(jnp.float32), kv)
    q_w = jnp.exp((i + 1.0)[:, None] * log_g[None, :])
    o_cross = jnp.einsum('bclhd,cbhde->bclhe', qc * q_w[:, :, None], s_prev)
    return (o_inner + o_cross).reshape(B, T, H, dv).astype(v.dtype), s_fin


def token_mixers(h, past_ak, past_av, past_ck, past_cv, s0, ret_chunk, q_block, lam_init,
                 w_in, qn_a_g, kn_a_g, lam_q1, lam_k1, lam_q2, lam_k2, subln_a_g, qn_c_g, kn_c_g,
                 rel_bias, w_up_a, w_up_b, w_up_c, w_o):
    B, Tq, _ = h.shape
    P = past_ak.shape[1]
    pos = P + jnp.arange(Tq, dtype=jnp.int32)
    offsets = np.cumsum(SPLITS)[:-1].tolist()
    qa, ka, va, qb, kb, vb, gb, qc, kc, vc, gl = jnp.split(h @ w_in, offsets, axis=-1)

    qa = rms_norm(qa.reshape(B, Tq, H_A, 2, HEAD_DIM), qn_a_g)
    ka = rms_norm(ka.reshape(B, Tq, H_A, 2, HEAD_DIM), kn_a_g)
    va = va.reshape(B, Tq, H_A, 2 * HEAD_DIM)
    k_all_a = jnp.concatenate([past_ak.reshape(B, P, H_A, 2, HEAD_DIM), ka], axis=1)
    v_all_a = jnp.concatenate([past_av, va], axis=1)
    lam = (jnp.exp(jnp.sum(lam_q1.astype(jnp.float32) * lam_k1.astype(jnp.float32)))
           - jnp.exp(jnp.sum(lam_q2.astype(jnp.float32) * lam_k2.astype(jnp.float32))) + lam_init)
    bias_a = rel_bias[:, :H_A].T
    oa = map_query_blocks(lambda qq, pp: diff_attend(qq, pp, k_all_a, v_all_a, lam, bias_a), qa, pos, q_block)
    oa = rms_norm(oa, subln_a_g) * (1.0 - lam_init)
    ya = oa.reshape(B, Tq, W_A) @ w_up_a

    qb = rotary(qb.reshape(B, Tq, H_B, DK_B), pos)
    kb = rotary(kb.reshape(B, Tq, H_B, DK_B), pos) * (DK_B ** -0.5)
    vb = vb.reshape(B, Tq, H_B, DV_B)
    ob, s_new = retention(qb, kb, vb, s0, ret_chunk)
    yb = (rms_norm_plain(ob).reshape(B, Tq, W_BV) * jax.nn.silu(gb)) @ w_up_b

    qc = rms_norm(qc.reshape(B, Tq, H_C, HEAD_DIM), qn_c_g)
    kc = rms_norm(kc.reshape(B, Tq, H_C, HEAD_DIM), kn_c_g)
    vc = vc.reshape(B, Tq, H_C, HEAD_DIM)
    T = P + Tq
    pad = (-T) % MOBA_BLOCK
    n_blk = (T + pad) // MOBA_BLOCK
    zpad = jnp.zeros((B, pad, H_C, HEAD_DIM), kc.dtype)
    kblk = jnp.concatenate([past_ck, kc, zpad], axis=1).reshape(B, n_blk, MOBA_BLOCK, H_C, HEAD_DIM).transpose(0, 3, 1, 2, 4)
    vblk = jnp.concatenate([past_cv, vc, zpad], axis=1).reshape(B, n_blk, MOBA_BLOCK, H_C, HEAD_DIM).transpose(0, 3, 1, 2, 4)
    k_mean = jnp.mean(kblk.astype(jnp.float32), axis=3)
    bias_c = rel_bias[:, H_A:].T
    oc = map_query_blocks(lambda qq, pp: moba_attend(qq, pp, kblk, vblk, k_mean, bias_c), qc, pos, q_block)
    yc = oc.reshape(B, Tq, W_C) @ w_up_c

    gates = jax.nn.sigmoid(gl.reshape(B, Tq, 3, D_MODEL))
    y = (gates[:, :, 0] * ya + gates[:, :, 1] * yb + gates[:, :, 2] * yc) @ w_o
    rows = (ka.reshape(B, Tq, H_A, 2 * HEAD_DIM), va, kc, vc, s_new)
    return y, rows


def gather_pages(cache, l, page_table):
    g = cache[l, page_table]
    return g.reshape((g.shape[0], g.shape[1] * g.shape[2]) + g.shape[3:])


def trunk(x, past_fn, state_fn, ret_chunk, q_block, weights):
    (ln_ffn1_g, ffn1_w_gate, ffn1_w_up, ffn1_w_down, ln_mix_g, w_in, qn_a_g, kn_a_g,
     lam_q1, lam_k1, lam_q2, lam_k2, subln_a_g, qn_c_g, kn_c_g, rel_bias,
     w_up_a, w_up_b, w_up_c, w_o, ln_ffn2_g, ffn2_w_gate, ffn2_w_up, ffn2_w_down) = weights
    layer_rows = []
    for l in range(DEPTH):
        x = x + 0.5 * swiglu(rms_norm(x, ln_ffn1_g[l]), ffn1_w_gate[l], ffn1_w_up[l], ffn1_w_down[l])
        past_ak, past_av, past_ck, past_cv = past_fn(l)
        lam_init = 0.8 - 0.6 * math.exp(-0.3 * l)
        mix, rows = token_mixers(rms_norm(x, ln_mix_g[l]), past_ak, past_av, past_ck, past_cv, state_fn(l),
                                 ret_chunk, q_block, lam_init, w_in[l], qn_a_g[l], kn_a_g[l],
                                 lam_q1[l], lam_k1[l], lam_q2[l], lam_k2[l], subln_a_g[l], qn_c_g[l], kn_c_g[l],
                                 rel_bias, w_up_a[l], w_up_b[l], w_up_c[l], w_o[l])
        x = x + mix
        x = x + 0.5 * swiglu(rms_norm(x, ln_ffn2_g[l]), ffn2_w_gate[l], ffn2_w_up[l], ffn2_w_down[l])
        layer_rows.append(rows)
    new_state = [jnp.stack([r[i] for r in layer_rows]) for i in range(len(layer_rows[0]))]
    return x, new_state


def setup_inputs(seed: int = 0) -> dict:
    key = jax.random.key(seed)
    keys = iter(jax.random.split(key, 40))
    f32 = jnp.float32
    n_pages = PAST_LEN // PAGE_SIZE
    n_used = DEC_BATCH * n_pages
    n_pool = n_used + max(1, n_used // 4)

    def nrm(shape, scale=1.0):
        return jax.random.normal(next(keys), shape, f32) * scale

    def gain(shape):
        return 1.0 + 0.02 * jax.random.normal(next(keys), shape, f32)

    inp = {}
    inp['x_prompt'] = nrm((BATCH, SEQ, D_MODEL))
    inp['x_sample'] = nrm((DEC_BATCH, DEC_SEQ, D_MODEL))
    inp['cache_a_k'] = nrm((DEPTH, n_pool, PAGE_SIZE, H_A, 2 * HEAD_DIM))
    inp['cache_a_v'] = nrm((DEPTH, n_pool, PAGE_SIZE, H_A, 2 * HEAD_DIM))
    inp['cache_c_k'] = nrm((DEPTH, n_pool, PAGE_SIZE, H_C, HEAD_DIM))
    inp['cache_c_v'] = nrm((DEPTH, n_pool, PAGE_SIZE, H_C, HEAD_DIM))
    inp['state_ret'] = nrm((DEPTH, DEC_BATCH, H_B, DK_B, DV_B))
    inp['page_table'] = jax.random.permutation(next(keys), n_pool)[:n_used].reshape(DEC_BATCH, n_pages).astype(jnp.int32)
    inp['ln_ffn1_g'] = gain((DEPTH, D_MODEL))
    inp['ffn1_w_gate'] = nrm((DEPTH, D_MODEL, D_FF), D_MODEL ** -0.5)
    inp['ffn1_w_up'] = nrm((DEPTH, D_MODEL, D_FF), D_MODEL ** -0.5)
    inp['ffn1_w_down'] = nrm((DEPTH, D_FF, D_MODEL), D_FF ** -0.5)
    inp['ln_mix_g'] = gain((DEPTH, D_MODEL))
    inp['w_in'] = nrm((DEPTH, D_MODEL, IN_WIDTH), D_MODEL ** -0.5)
    inp['qn_a_g'] = gain((DEPTH, HEAD_DIM))
    inp['kn_a_g'] = gain((DEPTH, HEAD_DIM))
    inp['lam_q1'] = nrm((DEPTH, HEAD_DIM), 0.1)
    inp['lam_k1'] = nrm((DEPTH, HEAD_DIM), 0.1)
    inp['lam_q2'] = nrm((DEPTH, HEAD_DIM), 0.1)
    inp['lam_k2'] = nrm((DEPTH, HEAD_DIM), 0.1)
    inp['subln_a_g'] = gain((DEPTH, 2 * HEAD_DIM))
    inp['qn_c_g'] = gain((DEPTH, HEAD_DIM))
    inp['kn_c_g'] = gain((DEPTH, HEAD_DIM))
    inp['rel_bias'] = nrm((N_BUCKETS, H_A + H_C), 0.5)
    inp['w_up_a'] = nrm((DEPTH, W_A, D_MODEL), W_A ** -0.5)
    inp['w_up_b'] = nrm((DEPTH, W_BV, D_MODEL), W_BV ** -0.5)
    inp['w_up_c'] = nrm((DEPTH, W_C, D_MODEL), W_C ** -0.5)
    inp['w_o'] = nrm((DEPTH, D_MODEL, D_MODEL), D_MODEL ** -0.5)
    inp['ln_ffn2_g'] = gain((DEPTH, D_MODEL))
    inp['ffn2_w_gate'] = nrm((DEPTH, D_MODEL, D_FF), D_MODEL ** -0.5)
    inp['ffn2_w_up'] = nrm((DEPTH, D_MODEL, D_FF), D_MODEL ** -0.5)
    inp['ffn2_w_down'] = nrm((DEPTH, D_FF, D_MODEL), D_FF ** -0.5)
    return inp


def reference(x_prompt, x_sample, cache_a_k, cache_a_v, cache_c_k, cache_c_v, state_ret, page_table,
              ln_ffn1_g, ffn1_w_gate, ffn1_w_up, ffn1_w_down, ln_mix_g, w_in, qn_a_g, kn_a_g,
              lam_q1, lam_k1, lam_q2, lam_k2, subln_a_g, qn_c_g, kn_c_g, rel_bias,
              w_up_a, w_up_b, w_up_c, w_o, ln_ffn2_g, ffn2_w_gate, ffn2_w_up, ffn2_w_down):
    weights = (ln_ffn1_g, ffn1_w_gate, ffn1_w_up, ffn1_w_down, ln_mix_g, w_in, qn_a_g, kn_a_g,
               lam_q1, lam_k1, lam_q2, lam_k2, subln_a_g, qn_c_g, kn_c_g, rel_bias,
               w_up_a, w_up_b, w_up_c, w_o, ln_ffn2_g, ffn2_w_gate, ffn2_w_up, ffn2_w_down)
    b = x_prompt.shape[0]
    empty_a = jnp.zeros((b, 0, H_A, 2 * HEAD_DIM), x_prompt.dtype)
    empty_c = jnp.zeros((b, 0, H_C, HEAD_DIM), x_prompt.dtype)
    zero_state = jnp.zeros((b, H_B, DK_B, DV_B), jnp.float32)
    y_prompt, (a_k_p, a_v_p, c_k_p, c_v_p, ret_p) = trunk(
        x_prompt, lambda l: (empty_a, empty_a, empty_c, empty_c), lambda l: zero_state,
        RET_CHUNK, Q_BLOCK, weights)
    y_sample, (a_k_s, a_v_s, c_k_s, c_v_s, ret_s) = trunk(
        x_sample,
        lambda l: (gather_pages(cache_a_k, l, page_table), gather_pages(cache_a_v, l, page_table),
                   gather_pages(cache_c_k, l, page_table), gather_pages(cache_c_v, l, page_table)),
        lambda l: state_ret[l], x_sample.shape[1], 1, weights)
    return (y_prompt, y_sample, a_k_p, a_v_p, c_k_p, c_v_p, ret_p, a_k_s, a_v_s, c_k_s, c_v_s, ret_s)
```

```python
import functools
import math

import jax
import jax.numpy as jnp
import numpy as np
from jax import lax
from jax.experimental import pallas as pl
from jax.experimental.pallas import tpu as pltpu

F32 = jnp.float32
BF16 = jnp.bfloat16

D_MODEL = 1024
D_FF = 2816
HEAD_DIM = 64
H_A = 4
H_B = 4
DK_B = 64
DV_B = 128
H_C = 8
RET_CHUNK = 128
MOBA_BLOCK = 256
MOBA_TOPK = 3
N_BUCKETS = 32
MAX_DISTANCE = 128
NORM_EPS = 1e-6
NEG = -1e30
W_A = H_A * 2 * HEAD_DIM
W_BQK = H_B * DK_B
W_BV = H_B * DV_B
W_C = H_C * HEAD_DIM
QK_SCALE = HEAD_DIM ** -0.5

LANES = 128
VMEM_LIMIT = 56 * 1024 * 1024
ROW_TILE = 512
ATT_A_TILE = 512
ATT_C_TQ = 512
PAGES_PER_STEP = 8


def _cparams(sem):
    return pltpu.CompilerParams(dimension_semantics=sem, vmem_limit_bytes=VMEM_LIMIT)


def _rms_rows(x, g):
    ms = jnp.mean(x * x, axis=-1, keepdims=True)
    return x * lax.rsqrt(ms + NORM_EPS) * g


def _dot_nt(a, b, **kw):
    return lax.dot_general(a, b, (((1,), (1,)), ((), ())), preferred_element_type=F32, **kw)


def _t5_thresholds():
    n = np.arange(0, 4 * MAX_DISTANCE)
    max_exact = N_BUCKETS // 2
    nf = np.maximum(n, 1).astype(np.float32)
    large = max_exact + (np.log(nf / np.float32(max_exact)) / np.float32(math.log(MAX_DISTANCE / max_exact))
                         * np.float32(N_BUCKETS - max_exact)).astype(np.int32)
    bucket = np.where(n < max_exact, n, np.minimum(large, N_BUCKETS - 1))
    return [int(np.argmax(bucket >= t)) for t in range(N_BUCKETS)]


T5_THR = _t5_thresholds()
T5_FAR = T5_THR[-1]


def _ffn_body(x_ref, g_ref, wg_ref, wu_ref, wd_ref, o_ref, h_scr, acc_scr):
    j = pl.program_id(1)

    @pl.when(j == 0)
    def _():
        h_scr[...] = _rms_rows(x_ref[...], g_ref[...]).astype(BF16)
        acc_scr[...] = jnp.zeros_like(acc_scr)

    h = h_scr[...]
    a = jnp.dot(h, wg_ref[...], preferred_element_type=F32)
    b = jnp.dot(h, wu_ref[...], preferred_element_type=F32)
    z = a * jax.nn.sigmoid(a) * b
    acc_scr[...] += jnp.dot(z.astype(BF16), wd_ref[...], preferred_element_type=F32)

    @pl.when(j == pl.num_programs(1) - 1)
    def _():
        o_ref[...] = x_ref[...] + 0.5 * acc_scr[...]


def _ffn(x, g, wg, wu, wd):
    m, d = x.shape
    f = wg.shape[1]
    tm = min(ROW_TILE, m)
    tf = f // 2
    assert m % tm == 0 and f % tf == 0 and tf % LANES == 0
    return pl.pallas_call(
        _ffn_body,
        grid=(m // tm, f // tf),
        in_specs=[
            pl.BlockSpec((tm, d), lambda i, j: (i, 0)),
            pl.BlockSpec((1, d), lambda i, j: (0, 0)),
            pl.BlockSpec((d, tf), lambda i, j: (0, j)),
            pl.BlockSpec((d, tf), lambda i, j: (0, j)),
            pl.BlockSpec((tf, d), lambda i, j: (j, 0)),
        ],
        out_specs=pl.BlockSpec((tm, d), lambda i, j: (i, 0)),
        out_shape=jax.ShapeDtypeStruct((m, d), F32),
        scratch_shapes=[pltpu.VMEM((tm, d), BF16), pltpu.VMEM((tm, d), F32)],
        compiler_params=_cparams(("parallel", "arbitrary")),
        name="ffn",
    )(x, g, wg, wu, wd)


def _proj_body(*refs, segs, n_norm, has_rot):
    it = iter(refs)
    x_ref, g_ref, w_ref = next(it), next(it), next(it)
    gains_ref = bd_ref = cos_ref = sin_ref = None
    if n_norm:
        gains_ref, bd_ref = next(it), next(it)
    if has_rot:
        cos_ref, sin_ref = next(it), next(it)
    outs = list(it)
    h = _rms_rows(x_ref[...], g_ref[...]).astype(BF16)
    off = oi = ni = 0
    for width, kind, scale_f32, scale_bf16, want_f32, want_bf16 in segs:
        y = jnp.dot(h, w_ref[:, off:off + width], preferred_element_type=F32)
        off += width
        if kind == "norm":
            y2 = y * y
            hi = y2.astype(BF16)
            lo = (y2 - hi.astype(F32)).astype(BF16)
            ss = (jnp.dot(hi, bd_ref[...], preferred_element_type=F32)
                  + jnp.dot(lo, bd_ref[...], preferred_element_type=F32))
            y = y * lax.rsqrt(ss * (1.0 / HEAD_DIM) + NORM_EPS) * gains_ref[ni:ni + 1, :]
            ni += 1
        elif kind == "rot":
            lane = lax.broadcasted_iota(jnp.int32, y.shape, 1)
            first_half = (lane % HEAD_DIM) < (HEAD_DIM // 2)
            swapped = jnp.where(first_half, pltpu.roll(y, width - HEAD_DIM // 2, 1),
                                pltpu.roll(y, HEAD_DIM // 2, 1))
            y = y * cos_ref[...] + swapped * sin_ref[...]
        if scale_f32 != 1.0:
            y = y * scale_f32
        if want_f32:
            outs[oi][...] = y
            oi += 1
        if want_bf16:
            outs[oi][...] = (y * scale_bf16).astype(BF16) if scale_bf16 != 1.0 else y.astype(BF16)
            oi += 1


def _proj(x, g, w, segs, gains=None, bd=None, cos=None, sin=None):
    m, d = x.shape
    tm = min(ROW_TILE, m)
    assert m % tm == 0
    wtot = w.shape[1]
    n_norm = sum(1 for s in segs if s[1] == "norm")
    has_rot = any(s[1] == "rot" for s in segs)
    args = [x, g, w]
    in_specs = [
        pl.BlockSpec((tm, d), lambda i: (i, 0)),
        pl.BlockSpec((1, d), lambda i: (0, 0)),
        pl.BlockSpec((d, wtot), lambda i: (0, 0)),
    ]
    if n_norm:
        args += [gains, bd]
        in_specs += [pl.BlockSpec(gains.shape, lambda i: (0, 0)), pl.BlockSpec(bd.shape, lambda i: (0, 0))]
    if has_rot:
        args += [cos, sin]
        in_specs += [pl.BlockSpec((tm, cos.shape[1]), lambda i: (i, 0)),
                     pl.BlockSpec((tm, sin.shape[1]), lambda i: (i, 0))]
    out_shape, out_specs = [], []
    for width, _, _, _, want_f32, want_bf16 in segs:
        for want, dt in ((want_f32, F32), (want_bf16, BF16)):
            if want:
                out_shape.append(jax.ShapeDtypeStruct((m, width), dt))
                out_specs.append(pl.BlockSpec((tm, width), lambda i: (i, 0)))
    return pl.pallas_call(
        functools.partial(_proj_body, segs=segs, n_norm=n_norm, has_rot=has_rot),
        grid=(m // tm,),
        in_specs=in_specs,
        out_specs=out_specs,
        out_shape=out_shape,
        compiler_params=_cparams(("parallel",)),
        name="proj",
    )(*args)


def _bias_tile_body(offs_ref, rb_ref, o_ref, *, rmul, head0):
    k = pl.program_id(0)
    hd = head0 + pl.program_id(1)
    tq, tk = o_ref.shape
    r = lax.broadcasted_iota(jnp.int32, (tq, tk), 0) * rmul
    c = lax.broadcasted_iota(jnp.int32, (tq, tk), 1)
    rel = r - c + offs_ref[k]
    b_far = rb_ref[N_BUCKETS - 1, hd]
    val = jnp.full((tq, tk), rb_ref[0, hd] - b_far, F32)
    for t in range(1, N_BUCKETS):
        val = jnp.where(rel >= T5_THR[t], rb_ref[t, hd] - b_far, val)
    o_ref[...] = jnp.where(rel >= 0, val, NEG)


def _bias_tiles(rel_bias, offs, n_heads, head0, tq, tk, rmul):
    offs = jnp.asarray(np.asarray(offs, np.int32))
    nk = offs.shape[0]
    return pl.pallas_call(
        functools.partial(_bias_tile_body, rmul=rmul, head0=head0),
        grid_spec=pltpu.PrefetchScalarGridSpec(
            num_scalar_prefetch=1,
            grid=(nk, n_heads),
            in_specs=[pl.BlockSpec(memory_space=pltpu.SMEM)],
            out_specs=pl.BlockSpec((None, None, tq, tk), lambda k, h, offs: (k, h, 0, 0)),
        ),
        out_shape=jax.ShapeDtypeStruct((nk, n_heads, tq, tk), F32),
        compiler_params=_cparams(("arbitrary", "arbitrary")),
        name="bias_tiles",
    )(offs, rel_bias)


def _tile_schedule(t, tq, tk):
    offs_kind = {}
    far_off = 1 << 24
    qi, kj, kind, flags = [], [], [], []
    for i in range(t // tq):
        js = [j for j in range(t // tk) if j * tk <= i * tq + tq - 1]
        for n, j in enumerate(js):
            off = i * tq - j * tk
            if off - (tk - 1) >= T5_FAR:
                off = far_off
            kd = offs_kind.setdefault(off, len(offs_kind))
            qi.append(i)
            kj.append(j)
            kind.append(kd)
            flags.append((1 if n == 0 else 0) | (2 if n == len(js) - 1 else 0))
    offs = [o for o, _ in sorted(offs_kind.items(), key=lambda kv: kv[1])]
    as_i32 = lambda v: jnp.asarray(np.asarray(v, np.int32))
    return as_i32(qi), as_i32(kj), as_i32(kind), as_i32(flags), offs


def _lambda_value(lam_ref, lam_init):
    lv = lam_ref[...]
    s1 = jnp.sum(lv[0:1, :] * lv[1:2, :], axis=-1, keepdims=True)
    s2 = jnp.sum(lv[2:3, :] * lv[3:4, :], axis=-1, keepdims=True)
    return jnp.exp(s1) - jnp.exp(s2) + lam_init


def _attn_a_body(qi_ref, kj_ref, kind_ref, fl_ref, q_ref, k_ref, v_ref, b_ref, lam_ref, sg_ref, o_ref,
                 m_scr, l_scr, acc_scr, *, lam_init):
    t = pl.program_id(0)
    flag = fl_ref[t]

    @pl.when((flag & 1) != 0)
    def _():
        m_scr[...] = jnp.full_like(m_scr, NEG)
        l_scr[...] = jnp.zeros_like(l_scr)
        acc_scr[...] = jnp.zeros_like(acc_scr)

    hw = 2 * HEAD_DIM
    lane = lax.broadcasted_iota(jnp.int32, (q_ref.shape[0], hw), 1)
    for h in range(H_A):
        qh = q_ref[:, h * hw:(h + 1) * hw]
        kh = k_ref[:, h * hw:(h + 1) * hw]
        vh = v_ref[:, h * hw:(h + 1) * hw]
        for c in range(2):
            i = 2 * h + c
            comp = (lane < HEAD_DIM) if c == 0 else (lane >= HEAD_DIM)
            s = _dot_nt(jnp.where(comp, qh, jnp.zeros_like(qh)), kh) + b_ref[h]
            m_prev = m_scr[i]
            m_new = jnp.maximum(m_prev, jnp.max(s, axis=-1, keepdims=True))
            alpha = jnp.exp(m_prev - m_new)
            p = jnp.exp(s - m_new)
            l_scr[i] = alpha * l_scr[i] + jnp.sum(p, axis=-1, keepdims=True)
            acc_scr[i] = alpha * acc_scr[i] + jnp.dot(p.astype(BF16), vh, preferred_element_type=F32)
            m_scr[i] = m_new

    @pl.when((flag & 2) != 0)
    def _():
        lam = _lambda_value(lam_ref, lam_init)
        for h in range(H_A):
            o = acc_scr[2 * h] / l_scr[2 * h] - lam * (acc_scr[2 * h + 1] / l_scr[2 * h + 1])
            o = _rms_rows(o, sg_ref[...]) * (1.0 - lam_init)
            o_ref[:, h * hw:(h + 1) * hw] = o.astype(BF16)


def _attn_a(q, k, v, bias, sched, lam_vecs, subln_g, lam_init):
    t = q.shape[0]
    tq = tk = min(ATT_A_TILE, t)
    qi, kj, kind, flags, _ = sched
    n = qi.shape[0]
    return pl.pallas_call(
        functools.partial(_attn_a_body, lam_init=lam_init),
        grid_spec=pltpu.PrefetchScalarGridSpec(
            num_scalar_prefetch=4,
            grid=(n,),
            in_specs=[
                pl.BlockSpec((tq, W_A), lambda s, qi, kj, kd, fl: (qi[s], 0)),
                pl.BlockSpec((tk, W_A), lambda s, qi, kj, kd, fl: (kj[s], 0)),
                pl.BlockSpec((tk, W_A), lambda s, qi, kj, kd, fl: (kj[s], 0)),
                pl.BlockSpec((None, H_A, tq, tk), lambda s, qi, kj, kd, fl: (kd[s], 0, 0, 0)),
                pl.BlockSpec((4, HEAD_DIM), lambda s, qi, kj, kd, fl: (0, 0)),
                pl.BlockSpec((1, 2 * HEAD_DIM), lambda s, qi, kj, kd, fl: (0, 0)),
            ],
            out_specs=pl.BlockSpec((tq, W_A), lambda s, qi, kj, kd, fl: (qi[s], 0)),
            scratch_shapes=[pltpu.VMEM((2 * H_A, tq, 1), F32), pltpu.VMEM((2 * H_A, tq, 1), F32),
                            pltpu.VMEM((2 * H_A, tq, 2 * HEAD_DIM), F32)],
        ),
        out_shape=jax.ShapeDtypeStruct((t, W_A), BF16),
        compiler_params=_cparams(("arbitrary",)),
        name="attn_a",
    )(qi, kj, kind, flags, q, k, v, bias, lam_vecs, subln_g)


def _ret_constants(chunk):
    log_g = np.log(1.0 - 2.0 ** (-5.0 - np.arange(H_B, dtype=np.float64)))
    i = np.arange(chunk, dtype=np.float64)
    d = i[:, None] - i[None, :]
    decay = np.where(d >= 0, np.exp(np.maximum(d, 0.0)[None] * log_g[:, None, None]), 0.0)
    head_of_lane = np.arange(W_BQK) // DK_B
    q_w = np.exp((i[:, None] + 1.0) * log_g[head_of_lane][None, :])
    k_w_t = np.exp((chunk - 1 - i)[None, :] * log_g[head_of_lane][:, None])
    g_c = np.exp(chunk * log_g)[head_of_lane]
    g_c = np.broadcast_to(g_c.reshape(H_B // 2, 2 * DK_B, 1), (H_B // 2, 2 * DK_B, DV_B))
    f = lambda a: jnp.asarray(np.asarray(a, np.float32))
    return f(decay), f(q_w), f(k_w_t), f(g_c), np.exp(log_g)


def _ret_body(q_ref, k_ref, kt_ref, v_ref, gb_ref, s0_ref, dec_ref, qw_ref, kwt_ref, gc_ref,
              o_ref, sfin_ref, s_scr):
    n = pl.program_id(0)

    @pl.when(n == 0)
    def _():
        s_scr[...] = s0_ref[...]

    pw = 2 * DK_B
    lane = lax.broadcasted_iota(jnp.int32, (q_ref.shape[0], pw), 1)
    row = lax.broadcasted_iota(jnp.int32, (pw, kt_ref.shape[1]), 0)
    for g in range(H_B // 2):
        s_prev = s_scr[g]
        s_prev_bf = s_prev.astype(BF16)
        qp = q_ref[:, g * pw:(g + 1) * pw]
        kp = k_ref[:, g * pw:(g + 1) * pw].astype(BF16)
        qwp = qp * qw_ref[:, g * pw:(g + 1) * pw]
        ktw = kt_ref[g * pw:(g + 1) * pw, :] * kwt_ref[g * pw:(g + 1) * pw, :]
        upd = jnp.zeros_like(s_prev)
        for hh in range(2):
            h = 2 * g + hh
            in_head = (lane < DK_B) if hh == 0 else (lane >= DK_B)
            in_head_r = (row < DK_B) if hh == 0 else (row >= DK_B)
            vh = v_ref[:, h * DV_B:(h + 1) * DV_B].astype(BF16)
            sc = _dot_nt(jnp.where(in_head, qp, 0.0).astype(BF16), kp) * dec_ref[h]
            o = (jnp.dot(sc.astype(BF16), vh, preferred_element_type=F32)
                 + jnp.dot(jnp.where(in_head, qwp, 0.0).astype(BF16), s_prev_bf, preferred_element_type=F32))
            o = o * lax.rsqrt(jnp.mean(o * o, axis=-1, keepdims=True) + NORM_EPS)
            gate = gb_ref[:, h * DV_B:(h + 1) * DV_B]
            o_ref[:, h * DV_B:(h + 1) * DV_B] = (o * (gate * jax.nn.sigmoid(gate))).astype(BF16)
            upd = upd + jnp.dot(jnp.where(in_head_r, ktw, 0.0).astype(BF16), vh, preferred_element_type=F32)
        s_scr[g] = gc_ref[g] * s_prev + upd

    @pl.when(n == pl.num_programs(0) - 1)
    def _():
        sfin_ref[...] = s_scr[...]


def _retention(q, k, v, gb, s0_pairs, consts):
    t = q.shape[0]
    c = RET_CHUNK
    decay, q_w, k_w_t, g_c, _ = consts
    kt = k.T
    full = lambda a: pl.BlockSpec(a.shape, lambda n: (0,) * a.ndim)
    return pl.pallas_call(
        _ret_body,
        grid=(t // c,),
        in_specs=[
            pl.BlockSpec((c, W_BQK), lambda n: (n, 0)),
            pl.BlockSpec((c, W_BQK), lambda n: (n, 0)),
            pl.BlockSpec((W_BQK, c), lambda n: (0, n)),
            pl.BlockSpec((c, W_BV), lambda n: (n, 0)),
            pl.BlockSpec((c, W_BV), lambda n: (n, 0)),
            full(s0_pairs), full(decay), full(q_w), full(k_w_t), full(g_c),
        ],
        out_specs=[pl.BlockSpec((c, W_BV), lambda n: (n, 0)), full(s0_pairs)],
        out_shape=[jax.ShapeDtypeStruct((t, W_BV), BF16), jax.ShapeDtypeStruct(s0_pairs.shape, F32)],
        scratch_shapes=[pltpu.VMEM(s0_pairs.shape, F32)],
        compiler_params=_cparams(("arbitrary",)),
        name="retention",
    )(q, k, kt, v, gb, s0_pairs, decay, q_w, k_w_t, g_c)


def _kmean_body(k_ref, o_ref):
    nb = o_ref.shape[0]
    kk = k_ref[...].reshape(nb, MOBA_BLOCK, k_ref.shape[1])
    o_ref[...] = jnp.sum(kk, axis=1) * (1.0 / MOBA_BLOCK)


def _kmean(k):
    t, w = k.shape
    nblk = t // MOBA_BLOCK
    nb = 8
    assert nblk % nb == 0
    return pl.pallas_call(
        _kmean_body,
        grid=(nblk // nb,),
        in_specs=[pl.BlockSpec((nb * MOBA_BLOCK, w), lambda i: (i, 0))],
        out_specs=pl.BlockSpec((nb, w), lambda i: (i, 0)),
        out_shape=jax.ShapeDtypeStruct((nblk, w), F32),
        compiler_params=_cparams(("parallel",)),
        name="moba_kmean",
    )(k)


def _top_blocks(gate, valid):
    lane = lax.broadcasted_iota(jnp.int32, gate.shape, 1).astype(F32)
    g = jnp.where(valid, gate, -jnp.inf)
    picked = jnp.zeros(gate.shape, F32)
    idxs = []
    for _ in range(MOBA_TOPK):
        m = jnp.max(g, axis=-1, keepdims=True)
        cand = jnp.logical_and(g == m, m > -jnp.inf)
        idx = jnp.min(jnp.where(cand, lane, float(LANES)), axis=-1, keepdims=True)
        pick = lane == idx
        picked = jnp.where(pick, 1.0, picked)
        g = jnp.where(pick, -jnp.inf, g)
        idxs.append(idx)
    return picked, idxs


def _select_body(q_ref, km_ref, o_ref, *, pos0):
    tq = q_ref.shape[0]
    q = q_ref[...]
    km = km_ref[...]
    lane_q = lax.broadcasted_iota(jnp.int32, q.shape, 1)
    blk = lax.broadcasted_iota(jnp.int32, (tq, LANES), 1)
    qpos = pos0 + pl.program_id(0) * tq + lax.broadcasted_iota(jnp.int32, (tq, LANES), 0)
    past = blk < qpos // MOBA_BLOCK
    for h in range(H_C):
        qh = jnp.where(lane_q // HEAD_DIM == h, q, 0.0)
        gate = _dot_nt(qh, km, precision=lax.Precision.HIGHEST)
        picked, _ = _top_blocks(gate, past)
        o_ref[:, h * LANES:(h + 1) * LANES] = jnp.where(jnp.logical_and(past, picked == 0.0), NEG, 0.0)


def _select(q, kmean_pad, pos0):
    t, w = q.shape
    tq = min(ROW_TILE, t)
    return pl.pallas_call(
        functools.partial(_select_body, pos0=pos0),
        grid=(t // tq,),
        in_specs=[pl.BlockSpec((tq, w), lambda i: (i, 0)),
                  pl.BlockSpec(kmean_pad.shape, lambda i: (0, 0))],
        out_specs=pl.BlockSpec((tq, H_C * LANES), lambda i: (i, 0)),
        out_shape=jax.ShapeDtypeStruct((t, H_C * LANES), F32),
        compiler_params=_cparams(("parallel",)),
        name="moba_select",
    )(q, kmean_pad)


def _attn_c_body(qi_ref, kj_ref, kind_ref, fl_ref, q_ref, k_ref, v_ref, b_ref, sel_ref, o_ref,
                 m_scr, l_scr, acc_scr):
    t = pl.program_id(0)
    flag = fl_ref[t]
    j = kj_ref[t]

    @pl.when((flag & 1) != 0)
    def _():
        m_scr[...] = jnp.full_like(m_scr, NEG)
        l_scr[...] = jnp.zeros_like(l_scr)
        acc_scr[...] = jnp.zeros_like(acc_scr)

    pw = 2 * HEAD_DIM
    lane = lax.broadcasted_iota(jnp.int32, (q_ref.shape[0], pw), 1)
    for h in range(H_C):
        g, hh = h // 2, h % 2
        qp = q_ref[:, g * pw:(g + 1) * pw]
        kp = k_ref[:, g * pw:(g + 1) * pw]
        vp = v_ref[:, g * pw:(g + 1) * pw]
        in_head = (lane < HEAD_DIM) if hh == 0 else (lane >= HEAD_DIM)
        row_bias = jnp.sum(jnp.where(lane == j, sel_ref[:, h * LANES:(h + 1) * LANES], 0.0),
                           axis=-1, keepdims=True)
        s = _dot_nt(jnp.where(in_head, qp, jnp.zeros_like(qp)), kp) + b_ref[h] + row_bias
        m_prev = m_scr[h]
        m_new = jnp.maximum(m_prev, jnp.max(s, axis=-1, keepdims=True))
        alpha = jnp.exp(m_prev - m_new)
        p = jnp.exp(s - m_new)
        l_scr[h] = alpha * l_scr[h] + jnp.sum(p, axis=-1, keepdims=True)
        acc_scr[h] = alpha * acc_scr[h] + jnp.dot(p.astype(BF16), vp, preferred_element_type=F32)
        m_scr[h] = m_new

    @pl.when((flag & 2) != 0)
    def _():
        for g in range(H_C // 2):
            o = jnp.where(lane < HEAD_DIM, acc_scr[2 * g] / l_scr[2 * g], acc_scr[2 * g + 1] / l_scr[2 * g + 1])
            o_ref[:, g * pw:(g + 1) * pw] = o.astype(BF16)


def _attn_c(q, k, v, bias, sched, selbias):
    t = q.shape[0]
    tq = min(ATT_C_TQ, t)
    tk = MOBA_BLOCK
    qi, kj, kind, flags, _ = sched
    n = qi.shape[0]
    return pl.pallas_call(
        _attn_c_body,
        grid_spec=pltpu.PrefetchScalarGridSpec(
            num_scalar_prefetch=4,
            grid=(n,),
            in_specs=[
                pl.BlockSpec((tq, W_C), lambda s, qi, kj, kd, fl: (qi[s], 0)),
                pl.BlockSpec((tk, W_C), lambda s, qi, kj, kd, fl: (kj[s], 0)),
                pl.BlockSpec((tk, W_C), lambda s, qi, kj, kd, fl: (kj[s], 0)),
                pl.BlockSpec((None, H_C, tq, tk), lambda s, qi, kj, kd, fl: (kd[s], 0, 0, 0)),
                pl.BlockSpec((tq, H_C * LANES), lambda s, qi, kj, kd, fl: (qi[s], 0)),
            ],
            out_specs=pl.BlockSpec((tq, W_C), lambda s, qi, kj, kd, fl: (qi[s], 0)),
            scratch_shapes=[pltpu.VMEM((H_C, tq, 1), F32), pltpu.VMEM((H_C, tq, 1), F32),
                            pltpu.VMEM((H_C, tq, 2 * HEAD_DIM), F32)],
        ),
        out_shape=jax.ShapeDtypeStruct((t, W_C), BF16),
        compiler_params=_cparams(("arbitrary",)),
        name="attn_c",
    )(qi, kj, kind, flags, q, k, v, bias, selbias)


def _merge_body(x_ref, g_ref, wg_ref, oa_ref, ob_ref, oc_ref, wa_ref, wb_ref, wc_ref, wo_ref, o_ref):
    x = x_ref[...]
    h = _rms_rows(x, g_ref[...]).astype(BF16)
    mix = None
    for n, (o_in, w_up) in enumerate(((oa_ref, wa_ref), (ob_ref, wb_ref), (oc_ref, wc_ref))):
        y = jnp.dot(o_in[...], w_up[...], preferred_element_type=F32)
        gl = jnp.dot(h, wg_ref[:, n * D_MODEL:(n + 1) * D_MODEL], preferred_element_type=F32)
        term = jax.nn.sigmoid(gl) * y
        mix = term if mix is None else mix + term
    o_ref[...] = x + jnp.dot(mix.astype(BF16), wo_ref[...], preferred_element_type=F32)


def _merge(x, g, wg, oa, ob, oc, wa, wb, wc, wo):
    m, d = x.shape
    tm = min(ROW_TILE, m)
    rows = lambda w: pl.BlockSpec((tm, w), lambda i: (i, 0))
    full = lambda a: pl.BlockSpec(a.shape, lambda i: (0, 0))
    return pl.pallas_call(
        _merge_body,
        grid=(m // tm,),
        in_specs=[rows(d), full(g), full(wg), rows(W_A), rows(W_BV), rows(W_C),
                  full(wa), full(wb), full(wc), full(wo)],
        out_specs=rows(d),
        out_shape=jax.ShapeDtypeStruct((m, d), F32),
        compiler_params=_cparams(("parallel",)),
        name="merge",
    )(x, g, wg, oa, ob, oc, wa, wb, wc, wo)


def _dec_a_body(pt_ref, q_ref, *refs, npg, lam_init):
    k_refs, v_refs = refs[:npg], refs[npg:2 * npg]
    b_ref, knew_ref, vnew_ref, b0_ref, lam_ref, sg_ref, o_ref, m_scr, l_scr, acc_scr = refs[2 * npg:]
    p = pl.program_id(1)
    rows = m_scr.shape[0]
    page = k_refs[0].shape[0]

    @pl.when(p == 0)
    def _():
        m_scr[...] = jnp.full_like(m_scr, NEG)
        l_scr[...] = jnp.zeros_like(l_scr)
        acc_scr[...] = jnp.zeros_like(acc_scr)

    lane = lax.broadcasted_iota(jnp.int32, (rows, W_A), 1)
    row = lax.broadcasted_iota(jnp.int32, (rows, W_A), 0)
    qrows = jnp.where(lane // HEAD_DIM == row, jnp.broadcast_to(q_ref[...], (rows, W_A)), 0.0).astype(BF16)

    s = jnp.concatenate([_dot_nt(qrows, k_refs[i][...].astype(BF16)) for i in range(npg)], axis=1) + b_ref[...]
    m_prev = m_scr[...]
    m_new = jnp.maximum(m_prev, jnp.max(s, axis=-1, keepdims=True))
    alpha = jnp.exp(m_prev - m_new)
    pr = jnp.exp(s - m_new).astype(BF16)
    l_scr[...] = alpha * l_scr[...] + jnp.sum(pr.astype(F32), axis=-1, keepdims=True)
    acc = alpha * acc_scr[...]
    for i in range(npg):
        acc = acc + jnp.dot(pr[:, i * page:(i + 1) * page], v_refs[i][...].astype(BF16),
                            preferred_element_type=F32)
    acc_scr[...] = acc
    m_scr[...] = m_new

    @pl.when(p == pl.num_programs(1) - 1)
    def _():
        k_new = knew_ref[...].astype(BF16).astype(F32)
        v_new = vnew_ref[...].astype(BF16).astype(F32)
        s_new = jnp.sum(qrows.astype(F32) * k_new, axis=-1, keepdims=True) + b0_ref[...]
        m_fin = jnp.maximum(m_scr[...], s_new)
        a_fin = jnp.exp(m_scr[...] - m_fin)
        p_new = jnp.exp(s_new - m_fin).astype(BF16).astype(F32)
        l_fin = a_fin * l_scr[...] + p_new
        o = (a_fin * acc_scr[...] + p_new * v_new) / l_fin
        lam = _lambda_value(lam_ref, lam_init)
        coef = jnp.where(row % 2 == 0, 1.0, -lam)
        mine = jnp.logical_and(lane // (2 * HEAD_DIM) == row // 2, row < 2 * H_A)
        comb = jnp.sum(jnp.where(mine, o * coef, 0.0), axis=0, keepdims=True)
        hw = 2 * HEAD_DIM
        for h in range(H_A):
            o_ref[:, h * hw:(h + 1) * hw] = _rms_rows(comb[:, h * hw:(h + 1) * hw], sg_ref[...]) * (1.0 - lam_init)


def _dec_a(layer, page_table, q, cache_k, cache_v, bias, k_new, v_new, b0, lam_vecs, subln_g, lam_init):
    bsz, n_pages = page_table.shape
    page = cache_k.shape[2]
    npg = min(PAGES_PER_STEP, n_pages)
    assert n_pages % npg == 0
    rows = 16
    seq_row = pl.BlockSpec((None, 1, W_A), lambda b, p, pt: (b, 0, 0))
    const = lambda a: pl.BlockSpec(a.shape, lambda b, p, pt: (0,) * a.ndim)

    def page_spec(i):
        return pl.BlockSpec((None, None, page, W_A), lambda b, p, pt: (layer, pt[b, p * npg + i], 0, 0))

    in_specs = ([seq_row] + [page_spec(i) for i in range(npg)] + [page_spec(i) for i in range(npg)]
                + [pl.BlockSpec((None, rows, npg * page), lambda b, p, pt: (p, 0, 0)),
                   seq_row, seq_row, const(b0), const(lam_vecs), const(subln_g)])
    return pl.pallas_call(
        functools.partial(_dec_a_body, npg=npg, lam_init=lam_init),
        grid_spec=pltpu.PrefetchScalarGridSpec(
            num_scalar_prefetch=1,
            grid=(bsz, n_pages // npg),
            in_specs=in_specs,
            out_specs=pl.BlockSpec((None, 1, W_A), lambda b, p, pt: (b, 0, 0)),
            scratch_shapes=[pltpu.VMEM((rows, 1), F32), pltpu.VMEM((rows, 1), F32), pltpu.VMEM((rows, W_A), F32)],
        ),
        out_shape=jax.ShapeDtypeStruct((bsz, 1, W_A), F32),
        compiler_params=_cparams(("parallel", "arbitrary")),
        name="decode_attn_a",
    )(page_table, q, *([cache_k] * npg), *([cache_v] * npg), bias, k_new, v_new, b0, lam_vecs, subln_g)


def _dec_ret_body(qc_ref, kc_ref, v_ref, gb_ref, s0_ref, o_ref, s_ref, *, gammas):
    for h in range(H_B):
        vh = v_ref[:, h * DV_B:(h + 1) * DV_B]
        s_new = float(gammas[h]) * s0_ref[h] + kc_ref[h] * vh
        s_ref[h] = s_new
        o = jnp.sum(qc_ref[h] * s_new, axis=0, keepdims=True)
        o = o * lax.rsqrt(jnp.mean(o * o, axis=-1, keepdims=True) + NORM_EPS)
        gate = gb_ref[:, h * DV_B:(h + 1) * DV_B]
        o_ref[:, h * DV_B:(h + 1) * DV_B] = o * (gate * jax.nn.sigmoid(gate))


def _dec_ret(layer, q, k, v, gb, state, gammas):
    bsz = q.shape[0]
    qc = q.reshape(bsz, H_B, DK_B, 1)
    kc = k.reshape(bsz, H_B, DK_B, 1)
    col = pl.BlockSpec((None, H_B, DK_B, 1), lambda b: (b, 0, 0, 0))
    row = pl.BlockSpec((None, 1, W_BV), lambda b: (b, 0, 0))
    return pl.pallas_call(
        functools.partial(_dec_ret_body, gammas=gammas),
        grid=(bsz,),
        in_specs=[col, col, row, row,
                  pl.BlockSpec((None, None, H_B, DK_B, DV_B), lambda b: (layer, b, 0, 0, 0))],
        out_specs=[row, pl.BlockSpec((None, H_B, DK_B, DV_B), lambda b: (b, 0, 0, 0))],
        out_shape=[jax.ShapeDtypeStruct((bsz, 1, W_BV), F32),
                   jax.ShapeDtypeStruct((bsz, H_B, DK_B, DV_B), F32)],
        compiler_params=_cparams(("parallel",)),
        name="decode_retention",
    )(qc, kc, v.reshape(bsz, 1, W_BV), gb.reshape(bsz, 1, W_BV), state)


def _dec_c1_body(pt_ref, q_ref, *refs, npg, ppb, n_past):
    k_refs = refs[:npg]
    o_ref, km_scr = refs[npg:]
    p = pl.program_id(1)

    @pl.when(p == 0)
    def _():
        km_scr[...] = jnp.zeros_like(km_scr)

    for i in range(npg // ppb):
        tot = None
        for u in range(ppb):
            part = jnp.sum(k_refs[i * ppb + u][...], axis=0, keepdims=True)
            tot = part if tot is None else tot + part
        km_scr[pl.ds(p * (npg // ppb) + i, 1), :] = tot * (1.0 / MOBA_BLOCK)

    @pl.when(p == pl.num_programs(1) - 1)
    def _():
        lane = lax.broadcasted_iota(jnp.int32, (H_C, W_C), 1)
        row = lax.broadcasted_iota(jnp.int32, (H_C, W_C), 0)
        qrows = jnp.where(lane // HEAD_DIM == row, jnp.broadcast_to(q_ref[...], (H_C, W_C)), 0.0)
        gate = _dot_nt(qrows, km_scr[...], precision=lax.Precision.HIGHEST)
        blk = lax.broadcasted_iota(jnp.int32, gate.shape, 1)
        _, idxs = _top_blocks(gate, blk < n_past)
        out = jnp.full(gate.shape, -1, jnp.int32)
        for r, idx in enumerate(idxs):
            out = jnp.where(jnp.logical_and(blk == r, idx < float(LANES)), idx.astype(jnp.int32), out)
        o_ref[...] = out


def _dec_c1(layer, page_table, q, cache_k):
    bsz, n_pages = page_table.shape
    page = cache_k.shape[2]
    ppb = MOBA_BLOCK // page
    npg = min(PAGES_PER_STEP, n_pages)
    assert MOBA_BLOCK % page == 0 and npg % ppb == 0 and n_pages % npg == 0
    n_past = n_pages // ppb
    assert n_past <= LANES

    def page_spec(i):
        return pl.BlockSpec((None, None, page, W_C), lambda b, p, pt: (layer, pt[b, p * npg + i], 0, 0))

    return pl.pallas_call(
        functools.partial(_dec_c1_body, npg=npg, ppb=ppb, n_past=n_past),
        grid_spec=pltpu.PrefetchScalarGridSpec(
            num_scalar_prefetch=1,
            grid=(bsz, n_pages // npg),
            in_specs=[pl.BlockSpec((None, 1, W_C), lambda b, p, pt: (b, 0, 0))] + [page_spec(i) for i in range(npg)],
            out_specs=pl.BlockSpec((None, H_C, LANES), lambda b, p, pt: (b, 0, 0)),
            scratch_shapes=[pltpu.VMEM((LANES, W_C), F32)],
        ),
        out_shape=jax.ShapeDtypeStruct((bsz, H_C, LANES), jnp.int32),
        compiler_params=_cparams(("parallel", "arbitrary")),
        name="decode_moba_select",
    )(page_table, q, *([cache_k] * npg))


def _dec_c2_body(pt_ref, idx_ref, q_ref, *refs, ppb):
    nk = MOBA_TOPK * ppb
    k_refs, v_refs, b_refs = refs[:nk], refs[nk:2 * nk], refs[2 * nk:2 * nk + MOBA_TOPK]
    knew_ref, vnew_ref, b0_ref, o_ref = refs[2 * nk + MOBA_TOPK:]
    b, h = pl.program_id(0), pl.program_id(1)
    pw = 2 * HEAD_DIM
    rows = 16
    lane = lax.broadcasted_iota(jnp.int32, (1, pw), 1)
    in_head = lane // HEAD_DIM == h % 2
    q = jnp.where(in_head, q_ref[...], 0.0)
    q_rows = jnp.broadcast_to(q, (rows, pw)).astype(BF16)
    ss, vs = [], []
    for r in range(MOBA_TOPK):
        valid = idx_ref[(b * H_C + h) * MOBA_TOPK + r] >= 0
        kb = jnp.concatenate([k_refs[r * ppb + u][...] for u in range(ppb)], axis=0).astype(BF16)
        vs.append(jnp.concatenate([v_refs[r * ppb + u][...] for u in range(ppb)], axis=0).astype(BF16))
        s = _dot_nt(q_rows, kb) + b_refs[r][...]
        ss.append(jnp.where(valid, s, NEG))
    k_new = knew_ref[...].astype(BF16).astype(F32)
    v_new = vnew_ref[...].astype(BF16).astype(F32)
    s_new = jnp.sum(q.astype(BF16).astype(F32) * k_new, axis=-1, keepdims=True) + b0_ref[...][:, 0:1]
    m = s_new
    for s in ss:
        m = jnp.maximum(m, jnp.max(s, axis=-1, keepdims=True))
    p_new = jnp.exp(s_new - m).astype(BF16).astype(F32)
    l = p_new
    o = p_new * v_new
    for s, v in zip(ss, vs):
        p = jnp.exp(s - m).astype(BF16)
        l = l + jnp.sum(p.astype(F32), axis=-1, keepdims=True)
        o = o + jnp.dot(p, v, preferred_element_type=F32)
    o_ref[...] = (o / l)[0:1, :]


def _dec_c2(layer, page_table, idx, q, cache_k, cache_v, bias_blocks, k_new, v_new, b0):
    bsz, n_pages = page_table.shape
    page = cache_k.shape[2]
    ppb = MOBA_BLOCK // page
    pw = 2 * HEAD_DIM
    idx_flat = idx[:, :, :MOBA_TOPK].reshape(-1)

    def blk_of(b, h, r, ix):
        return jnp.maximum(ix[(b * H_C + h) * MOBA_TOPK + r], 0)

    def page_spec(r, u):
        return pl.BlockSpec((None, None, page, pw),
                            lambda b, h, pt, ix: (layer, pt[b, blk_of(b, h, r, ix) * ppb + u], 0, h // 2))

    def bias_spec(r):
        return pl.BlockSpec((None, None, 1, MOBA_BLOCK), lambda b, h, pt, ix: (blk_of(b, h, r, ix), h, 0, 0))

    seq_pair = pl.BlockSpec((None, 1, pw), lambda b, h, pt, ix: (b, 0, h // 2))
    pages = [page_spec(r, u) for r in range(MOBA_TOPK) for u in range(ppb)]
    return pl.pallas_call(
        functools.partial(_dec_c2_body, ppb=ppb),
        grid_spec=pltpu.PrefetchScalarGridSpec(
            num_scalar_prefetch=2,
            grid=(bsz, H_C),
            in_specs=([seq_pair] + pages + pages + [bias_spec(r) for r in range(MOBA_TOPK)]
                      + [seq_pair, seq_pair, pl.BlockSpec((None, 1, pw), lambda b, h, pt, ix: (h, 0, 0))]),
            out_specs=pl.BlockSpec((None, None, 1, pw), lambda b, h, pt, ix: (b, h, 0, 0)),
        ),
        out_shape=jax.ShapeDtypeStruct((bsz, H_C, 1, pw), F32),
        compiler_params=_cparams(("parallel", "arbitrary")),
        name="decode_moba_attn",
    )(page_table, idx_flat, q, *([cache_k] * (MOBA_TOPK * ppb)), *([cache_v] * (MOBA_TOPK * ppb)),
      *([bias_blocks] * MOBA_TOPK), k_new, v_new, b0)


def _rotary_tables(pos):
    half = HEAD_DIM // 2
    inv = 1.0 / (10000.0 ** (jnp.arange(half, dtype=F32) / half))
    ang = pos.astype(F32)[:, None] * inv[None, :]
    cos, sin = jnp.cos(ang), jnp.sin(ang)
    cos_t = jnp.tile(jnp.concatenate([cos, cos], axis=1), (1, H_B))
    sin_t = jnp.tile(jnp.concatenate([-sin, sin], axis=1), (1, H_B))
    return cos_t, sin_t


def _block_diag_ones():
    i = np.arange(W_A) // HEAD_DIM
    return jnp.asarray((i[:, None] == i[None, :]).astype(np.float32), dtype=BF16)


SEG_A = ((W_A, "norm", 1.0, QK_SCALE, False, True),
         (W_A, "norm", 1.0, 1.0, True, True),
         (W_A, "plain", 1.0, 1.0, True, True))
SEG_B = ((W_BQK, "rot", 1.0, 1.0, True, False),
         (W_BQK, "rot", DK_B ** -0.5, 1.0, True, False),
         (W_BV, "plain", 1.0, 1.0, True, False),
         (W_BV, "plain", 1.0, 1.0, True, False))
SEG_C = ((W_C, "norm", 1.0, QK_SCALE, True, True),
         (W_C, "norm", 1.0, 1.0, True, True),
         (W_C, "plain", 1.0, 1.0, True, True))


def _layer_weights(l, w):
    bf = lambda a: a.astype(BF16)
    o_b, o_c, o_g = 3 * W_A, 3 * W_A + 2 * W_BQK + 2 * W_BV, 3 * W_A + 2 * W_BQK + 2 * W_BV + 3 * W_C
    w_in = w["w_in"][l]
    tile = lambda g: jnp.tile(g, W_A // HEAD_DIM)[None, :]
    return dict(
        ffn1=(w["ln_ffn1_g"][l][None, :], bf(w["ffn1_w_gate"][l]), bf(w["ffn1_w_up"][l]), bf(w["ffn1_w_down"][l])),
        ffn2=(w["ln_ffn2_g"][l][None, :], bf(w["ffn2_w_gate"][l]), bf(w["ffn2_w_up"][l]), bf(w["ffn2_w_down"][l])),
        ln_mix=w["ln_mix_g"][l][None, :],
        w_a=bf(w_in[:, :o_b]), w_b=bf(w_in[:, o_b:o_c]), w_c=bf(w_in[:, o_c:o_g]), w_g=bf(w_in[:, o_g:]),
        gains_a=jnp.concatenate([tile(w["qn_a_g"][l]), tile(w["kn_a_g"][l])], axis=0),
        gains_c=jnp.concatenate([tile(w["qn_c_g"][l]), tile(w["kn_c_g"][l])], axis=0),
        lam_vecs=jnp.stack([w["lam_q1"][l], w["lam_k1"][l], w["lam_q2"][l], w["lam_k2"][l]]),
        subln=w["subln_a_g"][l][None, :],
        w_up_a=bf(w["w_up_a"][l]), w_up_b=bf(w["w_up_b"][l]), w_up_c=bf(w["w_up_c"][l]), w_o=bf(w["w_o"][l]),
    )


def kernel(x_prompt, x_sample, cache_a_k, cache_a_v, cache_c_k, cache_c_v, state_ret, page_table, ln_ffn1_g, ffn1_w_gate, ffn1_w_up, ffn1_w_down, ln_mix_g, w_in, qn_a_g, kn_a_g, lam_q1, lam_k1, lam_q2, lam_k2, subln_a_g, qn_c_g, kn_c_g, rel_bias, w_up_a, w_up_b, w_up_c, w_o, ln_ffn2_g, ffn2_w_gate, ffn2_w_up, ffn2_w_down):
    w = dict(ln_ffn1_g=ln_ffn1_g, ffn1_w_gate=ffn1_w_gate, ffn1_w_up=ffn1_w_up, ffn1_w_down=ffn1_w_down,
             ln_mix_g=ln_mix_g, w_in=w_in, qn_a_g=qn_a_g, kn_a_g=kn_a_g, lam_q1=lam_q1, lam_k1=lam_k1,
             lam_q2=lam_q2, lam_k2=lam_k2, subln_a_g=subln_a_g, qn_c_g=qn_c_g, kn_c_g=kn_c_g,
             w_up_a=w_up_a, w_up_b=w_up_b, w_up_c=w_up_c, w_o=w_o, ln_ffn2_g=ln_ffn2_g,
             ffn2_w_gate=ffn2_w_gate, ffn2_w_up=ffn2_w_up, ffn2_w_down=ffn2_w_down)
    depth = w_in.shape[0]
    assert x_prompt.shape[0] == 1 and x_sample.shape[1] == 1
    t = x_prompt.shape[1]
    bsz, n_pages = page_table.shape
    page = cache_a_k.shape[2]
    n_pool = cache_a_k.shape[1]
    past = n_pages * page
    assert t % MOBA_BLOCK == 0 and past % MOBA_BLOCK == 0 and t % RET_CHUNK == 0

    bd = _block_diag_ones()
    cos_p, sin_p = _rotary_tables(jnp.arange(t, dtype=jnp.int32))
    cos_s, sin_s = _rotary_tables(jnp.full((bsz,), past, jnp.int32))
    ret_consts = _ret_constants(RET_CHUNK)
    gammas = ret_consts[4]

    ta = min(ATT_A_TILE, t)
    sched_a = _tile_schedule(t, ta, ta)
    bias_a = _bias_tiles(rel_bias, sched_a[4], H_A, 0, ta, ta, 1)
    tcq = min(ATT_C_TQ, t)
    sched_c = _tile_schedule(t, tcq, MOBA_BLOCK)
    bias_c = _bias_tiles(rel_bias, sched_c[4], H_C, H_A, tcq, MOBA_BLOCK, 1)

    npg = min(PAGES_PER_STEP, n_pages)
    dec_steps = n_pages // npg
    bias_da = _bias_tiles(rel_bias, [past - s * npg * page for s in range(dec_steps)], H_A, 0, 8, npg * page, 0)
    bias_da = jnp.repeat(bias_da[:, :, 0, :], 2, axis=1)
    bias_da = jnp.concatenate([bias_da, jnp.zeros_like(bias_da)], axis=1)
    n_past_blk = past // MOBA_BLOCK
    bias_dc = _bias_tiles(rel_bias, [past - j * MOBA_BLOCK for j in range(n_past_blk)], H_C, H_A, 8, MOBA_BLOCK, 0)
    bias_dc = bias_dc[:, :, 0:1, :]
    b_self = rel_bias[0] - rel_bias[N_BUCKETS - 1]
    b0_a = jnp.concatenate([jnp.repeat(b_self[:H_A], 2), jnp.zeros((8,), F32)])[:, None]
    b0_c = jnp.broadcast_to(b_self[H_A:, None, None], (H_C, 1, 2 * HEAD_DIM))

    ck_a = cache_a_k.reshape(depth, n_pool, page, W_A)
    cv_a = cache_a_v.reshape(depth, n_pool, page, W_A)
    ck_c = cache_c_k.reshape(depth, n_pool, page, W_C)
    cv_c = cache_c_v.reshape(depth, n_pool, page, W_C)

    xp = x_prompt[0]
    xs = x_sample[:, 0]
    rows_p, rows_s = [], []
    for l in range(depth):
        lw = _layer_weights(l, w)
        lam_init = 0.8 - 0.6 * math.exp(-0.3 * l)

        xp = _ffn(xp, *lw["ffn1"])
        qa_bf, ka, ka_bf, va, va_bf = _proj(xp, lw["ln_mix"], lw["w_a"], SEG_A, gains=lw["gains_a"], bd=bd)
        qb, kb, vb, gb = _proj(xp, lw["ln_mix"], lw["w_b"], SEG_B, cos=cos_p, sin=sin_p)
        qc, qc_bf, kc, kc_bf, vc, vc_bf = _proj(xp, lw["ln_mix"], lw["w_c"], SEG_C, gains=lw["gains_c"], bd=bd)
        oa = _attn_a(qa_bf, ka_bf, va_bf, bias_a, sched_a, lw["lam_vecs"], lw["subln"], lam_init)
        s0 = jnp.zeros((H_B // 2, 2 * DK_B, DV_B), F32)
        ob, s_fin = _retention(qb, kb, vb, gb, s0, ret_consts)
        kmean = _kmean(kc)
        kmean_pad = jnp.pad(kmean, ((0, LANES - kmean.shape[0]), (0, 0)))
        selbias = _select(qc, kmean_pad, 0)
        oc = _attn_c(qc_bf, kc_bf, vc_bf, bias_c, sched_c, selbias)
        xp = _merge(xp, lw["ln_mix"], lw["w_g"], oa, ob, oc, lw["w_up_a"], lw["w_up_b"], lw["w_up_c"], lw["w_o"])
        xp = _ffn(xp, *lw["ffn2"])
        rows_p.append((ka.reshape(1, t, H_A, 2 * HEAD_DIM), va.reshape(1, t, H_A, 2 * HEAD_DIM),
                       kc.reshape(1, t, H_C, HEAD_DIM), vc.reshape(1, t, H_C, HEAD_DIM),
                       s_fin.reshape(1, H_B, DK_B, DV_B)))

        xs = _ffn(xs, *lw["ffn1"])
        sqa_bf, ska, _, sva, _ = _proj(xs, lw["ln_mix"], lw["w_a"], SEG_A, gains=lw["gains_a"], bd=bd)
        sqb, skb, svb, sgb = _proj(xs, lw["ln_mix"], lw["w_b"], SEG_B, cos=cos_s, sin=sin_s)
        sqc, sqc_bf, skc, _, svc, _ = _proj(xs, lw["ln_mix"], lw["w_c"], SEG_C, gains=lw["gains_c"], bd=bd)
        r3 = lambda a: a.reshape(bsz, 1, a.shape[-1])
        soa = _dec_a(l, page_table, r3(sqa_bf.astype(F32)), ck_a, cv_a, bias_da, r3(ska), r3(sva), b0_a,
                     lw["lam_vecs"], lw["subln"], lam_init)
        sob, s_new = _dec_ret(l, sqb, skb, svb, sgb, state_ret, gammas)
        idx = _dec_c1(l, page_table, r3(sqc), ck_c)
        soc = _dec_c2(l, page_table, idx, r3(sqc_bf.astype(F32)), ck_c, cv_c, bias_dc, r3(skc), r3(svc), b0_c)
        soc = soc.reshape(bsz, H_C // 2, 2, 2, HEAD_DIM)
        soc = jnp.stack([soc[:, :, 0, 0, :], soc[:, :, 1, 1, :]], axis=2).reshape(bsz, W_C)
        xs = _merge(xs, lw["ln_mix"], lw["w_g"], soa.reshape(bsz, W_A).astype(BF16),
                    sob.reshape(bsz, W_BV).astype(BF16), soc.astype(BF16),
                    lw["w_up_a"], lw["w_up_b"], lw["w_up_c"], lw["w_o"])
        xs = _ffn(xs, *lw["ffn2"])
        rows_s.append((ska.reshape(bsz, 1, H_A, 2 * HEAD_DIM), sva.reshape(bsz, 1, H_A, 2 * HEAD_DIM),
                       skc.reshape(bsz, 1, H_C, HEAD_DIM), svc.reshape(bsz, 1, H_C, HEAD_DIM), s_new))

    stack = lambda rows, i: jnp.stack([r[i] for r in rows])
    return (xp[None], xs[:, None, :],
            stack(rows_p, 0), stack(rows_p, 1), stack(rows_p, 2), stack(rows_p, 3), stack(rows_p, 4),
            stack(rows_s, 0), stack(rows_s, 1), stack(rows_s, 2), stack(rows_s, 3), stack(rows_s, 4))
```

```python
import functools
import math

import jax
import jax.numpy as jnp
import numpy as np
from jax import lax
from jax.experimental import pallas as pl
from jax.experimental.pallas import tpu as pltpu

F32 = jnp.float32
BF16 = jnp.bfloat16

D_MODEL = 1024
D_FF = 2816
HEAD_DIM = 64
H_A = 4
H_B = 4
DK_B = 64
DV_B = 128
H_C = 8
RET_CHUNK = 128
MOBA_BLOCK = 256
MOBA_TOPK = 3
N_BUCKETS = 32
MAX_DISTANCE = 128
NORM_EPS = 1e-6
NEG = -1e30
W_A = H_A * 2 * HEAD_DIM
W_BQK = H_B * DK_B
W_BV = H_B * DV_B
W_C = H_C * HEAD_DIM
QK_SCALE = HEAD_DIM ** -0.5

LANES = 128
VMEM_LIMIT = 56 * 1024 * 1024
ROW_TILE = 512
ATT_A_TILE = 512
ATT_C_TQ = 512
PAGES_PER_STEP = 8


def _cparams(sem):
    return pltpu.CompilerParams(dimension_semantics=sem, vmem_limit_bytes=VMEM_LIMIT)


def _rms_rows(x, g):
    ms = jnp.mean(x * x, axis=-1, keepdims=True)
    return x * lax.rsqrt(ms + NORM_EPS) * g


def _dot_nt(a, b, **kw):
    return lax.dot_general(a, b, (((1,), (1,)), ((), ())), preferred_element_type=F32, **kw)


def _t5_thresholds():
    n = np.arange(0, 4 * MAX_DISTANCE)
    max_exact = N_BUCKETS // 2
    nf = np.maximum(n, 1).astype(np.float32)
    large = max_exact + (np.log(nf / np.float32(max_exact)) / np.float32(math.log(MAX_DISTANCE / max_exact))
                         * np.float32(N_BUCKETS - max_exact)).astype(np.int32)
    bucket = np.where(n < max_exact, n, np.minimum(large, N_BUCKETS - 1))
    return [int(np.argmax(bucket >= t)) for t in range(N_BUCKETS)]


T5_THR = _t5_thresholds()
T5_FAR = T5_THR[-1]


def _ffn_body(x_ref, g_ref, wg_ref, wu_ref, wd_ref, o_ref, h_scr, acc_scr):
    j = pl.program_id(1)

    @pl.when(j == 0)
    def _():
        h_scr[...] = _rms_rows(x_ref[...], g_ref[...]).astype(BF16)
        acc_scr[...] = jnp.zeros_like(acc_scr)

    h = h_scr[...]
    a = jnp.dot(h, wg_ref[...], preferred_element_type=F32)
    b = jnp.dot(h, wu_ref[...], preferred_element_type=F32)
    z = a * jax.nn.sigmoid(a) * b
    acc_scr[...] += jnp.dot(z.astype(BF16), wd_ref[...], preferred_element_type=F32)

    @pl.when(j == pl.num_programs(1) - 1)
    def _():
        o_ref[...] = x_ref[...] + 0.5 * acc_scr[...]


def _ffn(x, g, wg, wu, wd):
    m, d = x.shape
    f = wg.shape[1]
    tm = min(ROW_TILE, m)
    tf = f // 2
    assert m % tm == 0 and f % tf == 0 and tf % LANES == 0
    return pl.pallas_call(
        _ffn_body,
        grid=(m // tm, f // tf),
        in_specs=[
            pl.BlockSpec((tm, d), lambda i, j: (i, 0)),
            pl.BlockSpec((1, d), lambda i, j: (0, 0)),
            pl.BlockSpec((d, tf), lambda i, j: (0, j)),
            pl.BlockSpec((d, tf), lambda i, j: (0, j)),
            pl.BlockSpec((tf, d), lambda i, j: (j, 0)),
        ],
        out_specs=pl.BlockSpec((tm, d), lambda i, j: (i, 0)),
        out_shape=jax.ShapeDtypeStruct((m, d), F32),
        scratch_shapes=[pltpu.VMEM((tm, d), BF16), pltpu.VMEM((tm, d), F32)],
        compiler_params=_cparams(("parallel", "arbitrary")),
        name="ffn",
    )(x, g, wg, wu, wd)


def _proj_body(*refs, segs, n_norm, has_rot):
    it = iter(refs)
    x_ref, g_ref, w_ref = next(it), next(it), next(it)
    gains_ref = bd_ref = cos_ref = sin_ref = None
    if n_norm:
        gains_ref, bd_ref = next(it), next(it)
    if has_rot:
        cos_ref, sin_ref = next(it), next(it)
    outs = list(it)
    h = _rms_rows(x_ref[...], g_ref[...]).astype(BF16)
    off = oi = ni = 0
    for width, kind, scale_f32, scale_bf16, want_f32, want_bf16 in segs:
        y = jnp.dot(h, w_ref[:, off:off + width], preferred_element_type=F32)
        off += width
        if kind == "norm":
            y2 = y * y
            hi = y2.astype(BF16)
            lo = (y2 - hi.astype(F32)).astype(BF16)
            ss = (jnp.dot(hi, bd_ref[...], preferred_element_type=F32)
                  + jnp.dot(lo, bd_ref[...], preferred_element_type=F32))
            y = y * lax.rsqrt(ss * (1.0 / HEAD_DIM) + NORM_EPS) * gains_ref[ni:ni + 1, :]
            ni += 1
        elif kind == "rot":
            lane = lax.broadcasted_iota(jnp.int32, y.shape, 1)
            first_half = (lane % HEAD_DIM) < (HEAD_DIM // 2)
            swapped = jnp.where(first_half, pltpu.roll(y, width - HEAD_DIM // 2, 1),
                                pltpu.roll(y, HEAD_DIM // 2, 1))
            y = y * cos_ref[...] + swapped * sin_ref[...]
        if scale_f32 != 1.0:
            y = y * scale_f32
        if want_f32:
            outs[oi][...] = y
            oi += 1
        if want_bf16:
            outs[oi][...] = (y * scale_bf16).astype(BF16) if scale_bf16 != 1.0 else y.astype(BF16)
            oi += 1


def _proj(x, g, w, segs, gains=None, bd=None, cos=None, sin=None):
    m, d = x.shape
    tm = min(ROW_TILE, m)
    assert m % tm == 0
    wtot = w.shape[1]
    n_norm = sum(1 for s in segs if s[1] == "norm")
    has_rot = any(s[1] == "rot" for s in segs)
    args = [x, g, w]
    in_specs = [
        pl.BlockSpec((tm, d), lambda i: (i, 0)),
        pl.BlockSpec((1, d), lambda i: (0, 0)),
        pl.BlockSpec((d, wtot), lambda i: (0, 0)),
    ]
    if n_norm:
        args += [gains, bd]
        in_specs += [pl.BlockSpec(gains.shape, lambda i: (0, 0)), pl.BlockSpec(bd.shape, lambda i: (0, 0))]
    if has_rot:
        args += [cos, sin]
        in_specs += [pl.BlockSpec((tm, cos.shape[1]), lambda i: (i, 0)),
                     pl.BlockSpec((tm, sin.shape[1]), lambda i: (i, 0))]
    out_shape, out_specs = [], []
    for width, _, _, _, want_f32, want_bf16 in segs:
        for want, dt in ((want_f32, F32), (want_bf16, BF16)):
            if want:
                out_shape.append(jax.ShapeDtypeStruct((m, width), dt))
                out_specs.append(pl.BlockSpec((tm, width), lambda i: (i, 0)))
    return pl.pallas_call(
        functools.partial(_proj_body, segs=segs, n_norm=n_norm, has_rot=has_rot),
        grid=(m // tm,),
        in_specs=in_specs,
        out_specs=out_specs,
        out_shape=out_shape,
        compiler_params=_cparams(("parallel",)),
        name="proj",
    )(*args)


def _bias_tile_body(offs_ref, rb_ref, o_ref, *, rmul, head0):
    k = pl.program_id(0)
    hd = head0 + pl.program_id(1)
    tq, tk = o_ref.shape
    r = lax.broadcasted_iota(jnp.int32, (tq, tk), 0) * rmul
    c = lax.broadcasted_iota(jnp.int32, (tq, tk), 1)
    rel = r - c + offs_ref[k]
    b_far = rb_ref[N_BUCKETS - 1, hd]
    val = jnp.full((tq, tk), rb_ref[0, hd] - b_far, F32)
    for t in range(1, N_BUCKETS):
        val = jnp.where(rel >= T5_THR[t], rb_ref[t, hd] - b_far, val)
    o_ref[...] = jnp.where(rel >= 0, val, NEG)


def _bias_tiles(rel_bias, offs, n_heads, head0, tq, tk, rmul):
    offs = jnp.asarray(np.asarray(offs, np.int32))
    nk = offs.shape[0]
    return pl.pallas_call(
        functools.partial(_bias_tile_body, rmul=rmul, head0=head0),
        grid_spec=pltpu.PrefetchScalarGridSpec(
            num_scalar_prefetch=1,
            grid=(nk, n_heads),
            in_specs=[pl.BlockSpec(memory_space=pltpu.SMEM)],
            out_specs=pl.BlockSpec((None, None, tq, tk), lambda k, h, offs: (k, h, 0, 0)),
        ),
        out_shape=jax.ShapeDtypeStruct((nk, n_heads, tq, tk), F32),
        compiler_params=_cparams(("arbitrary", "arbitrary")),
        name="bias_tiles",
    )(offs, rel_bias)


def _tile_schedule(t, tq, tk):
    offs_kind = {}
    far_off = 1 << 24
    qi, kj, kind, flags = [], [], [], []
    for i in range(t // tq):
        js = [j for j in range(t // tk) if j * tk <= i * tq + tq - 1]
        for n, j in enumerate(js):
            off = i * tq - j * tk
            if off - (tk - 1) >= T5_FAR:
                off = far_off
            kd = offs_kind.setdefault(off, len(offs_kind))
            qi.append(i)
            kj.append(j)
            kind.append(kd)
            flags.append((1 if n == 0 else 0) | (2 if n == len(js) - 1 else 0))
    offs = [o for o, _ in sorted(offs_kind.items(), key=lambda kv: kv[1])]
    as_i32 = lambda v: jnp.asarray(np.asarray(v, np.int32))
    return as_i32(qi), as_i32(kj), as_i32(kind), as_i32(flags), offs


def _lambda_value(lam_ref, lam_init):
    lv = lam_ref[...]
    s1 = jnp.sum(lv[0:1, :] * lv[1:2, :], axis=-1, keepdims=True)
    s2 = jnp.sum(lv[2:3, :] * lv[3:4, :], axis=-1, keepdims=True)
    return jnp.exp(s1) - jnp.exp(s2) + lam_init


def _attn_a_body(qi_ref, kj_ref, kind_ref, fl_ref, q_ref, k_ref, v_ref, b_ref, lam_ref, sg_ref, o_ref,
                 m_scr, l_scr, acc_scr, *, lam_init):
    t = pl.program_id(0)
    flag = fl_ref[t]

    @pl.when((flag & 1) != 0)
    def _():
        m_scr[...] = jnp.full_like(m_scr, NEG)
        l_scr[...] = jnp.zeros_like(l_scr)
        acc_scr[...] = jnp.zeros_like(acc_scr)

    hw = 2 * HEAD_DIM
    reps = k_ref.shape[0] // LANES
    lane = lax.broadcasted_iota(jnp.int32, (q_ref.shape[0], hw), 1)
    for h in range(H_A):
        qh = q_ref[:, h * hw:(h + 1) * hw]
        kh = k_ref[:, h * hw:(h + 1) * hw]
        vh = v_ref[:, h * hw:(h + 1) * hw]
        for c in range(2):
            i = 2 * h + c
            comp = (lane < HEAD_DIM) if c == 0 else (lane >= HEAD_DIM)
            s = _dot_nt(jnp.where(comp, qh, jnp.zeros_like(qh)), kh) + b_ref[h]
            m_prev = m_scr[i]
            m_new = jnp.maximum(m_prev, jnp.max(s, axis=-1, keepdims=True))
            alpha = jnp.exp(m_prev - m_new)
            p = jnp.exp(s - jnp.concatenate([m_new] * reps, axis=1))
            l_scr[i] = alpha * l_scr[i] + jnp.sum(p, axis=-1, keepdims=True)
            acc_scr[i] = alpha * acc_scr[i] + jnp.dot(p.astype(BF16), vh, preferred_element_type=F32)
            m_scr[i] = m_new

    @pl.when((flag & 2) != 0)
    def _():
        lam = _lambda_value(lam_ref, lam_init)
        for h in range(H_A):
            o = acc_scr[2 * h] / l_scr[2 * h] - lam * (acc_scr[2 * h + 1] / l_scr[2 * h + 1])
            o = _rms_rows(o, sg_ref[...]) * (1.0 - lam_init)
            o_ref[:, h * hw:(h + 1) * hw] = o.astype(BF16)


def _attn_a(q, k, v, bias, sched, lam_vecs, subln_g, lam_init):
    t = q.shape[0]
    tq = tk = min(ATT_A_TILE, t)
    qi, kj, kind, flags, _ = sched
    n = qi.shape[0]
    return pl.pallas_call(
        functools.partial(_attn_a_body, lam_init=lam_init),
        grid_spec=pltpu.PrefetchScalarGridSpec(
            num_scalar_prefetch=4,
            grid=(n,),
            in_specs=[
                pl.BlockSpec((tq, W_A), lambda s, qi, kj, kd, fl: (qi[s], 0)),
                pl.BlockSpec((tk, W_A), lambda s, qi, kj, kd, fl: (kj[s], 0)),
                pl.BlockSpec((tk, W_A), lambda s, qi, kj, kd, fl: (kj[s], 0)),
                pl.BlockSpec((None, H_A, tq, tk), lambda s, qi, kj, kd, fl: (kd[s], 0, 0, 0)),
                pl.BlockSpec((4, HEAD_DIM), lambda s, qi, kj, kd, fl: (0, 0)),
                pl.BlockSpec((1, 2 * HEAD_DIM), lambda s, qi, kj, kd, fl: (0, 0)),
            ],
            out_specs=pl.BlockSpec((tq, W_A), lambda s, qi, kj, kd, fl: (qi[s], 0)),
            scratch_shapes=[pltpu.VMEM((2 * H_A, tq, LANES), F32), pltpu.VMEM((2 * H_A, tq, LANES), F32),
                            pltpu.VMEM((2 * H_A, tq, 2 * HEAD_DIM), F32)],
        ),
        out_shape=jax.ShapeDtypeStruct((t, W_A), BF16),
        compiler_params=_cparams(("arbitrary",)),
        name="attn_a",
    )(qi, kj, kind, flags, q, k, v, bias, lam_vecs, subln_g)


def _ret_constants(chunk):
    log_g = np.log(1.0 - 2.0 ** (-5.0 - np.arange(H_B, dtype=np.float64)))
    i = np.arange(chunk, dtype=np.float64)
    d = i[:, None] - i[None, :]
    decay = np.where(d >= 0, np.exp(np.maximum(d, 0.0)[None] * log_g[:, None, None]), 0.0)
    head_of_lane = np.arange(W_BQK) // DK_B
    q_w = np.exp((i[:, None] + 1.0) * log_g[head_of_lane][None, :])
    k_w_t = np.exp((chunk - 1 - i)[None, :] * log_g[head_of_lane][:, None])
    g_c = np.exp(chunk * log_g)[head_of_lane]
    g_c = np.broadcast_to(g_c.reshape(H_B // 2, 2 * DK_B, 1), (H_B // 2, 2 * DK_B, DV_B))
    f = lambda a: jnp.asarray(np.asarray(a, np.float32))
    return f(decay), f(q_w), f(k_w_t), f(g_c), np.exp(log_g)


def _ret_body(q_ref, k_ref, kt_ref, v_ref, gb_ref, s0_ref, dec_ref, qw_ref, kwt_ref, gc_ref,
              o_ref, sfin_ref, s_scr):
    n = pl.program_id(0)

    @pl.when(n == 0)
    def _():
        s_scr[...] = s0_ref[...]

    pw = 2 * DK_B
    lane = lax.broadcasted_iota(jnp.int32, (q_ref.shape[0], pw), 1)
    row = lax.broadcasted_iota(jnp.int32, (pw, kt_ref.shape[1]), 0)
    for g in range(H_B // 2):
        s_prev = s_scr[g]
        s_prev_bf = s_prev.astype(BF16)
        qp = q_ref[:, g * pw:(g + 1) * pw]
        kp = k_ref[:, g * pw:(g + 1) * pw].astype(BF16)
        qwp = qp * qw_ref[:, g * pw:(g + 1) * pw]
        ktw = kt_ref[g * pw:(g + 1) * pw, :] * kwt_ref[g * pw:(g + 1) * pw, :]
        upd = jnp.zeros_like(s_prev)
        for hh in range(2):
            h = 2 * g + hh
            in_head = (lane < DK_B) if hh == 0 else (lane >= DK_B)
            in_head_r = (row < DK_B) if hh == 0 else (row >= DK_B)
            vh = v_ref[:, h * DV_B:(h + 1) * DV_B].astype(BF16)
            sc = _dot_nt(jnp.where(in_head, qp, 0.0).astype(BF16), kp) * dec_ref[h]
            o = (jnp.dot(sc.astype(BF16), vh, preferred_element_type=F32)
                 + jnp.dot(jnp.where(in_head, qwp, 0.0).astype(BF16), s_prev_bf, preferred_element_type=F32))
            o = o * lax.rsqrt(jnp.mean(o * o, axis=-1, keepdims=True) + NORM_EPS)
            gate = gb_ref[:, h * DV_B:(h + 1) * DV_B]
            o_ref[:, h * DV_B:(h + 1) * DV_B] = (o * (gate * jax.nn.sigmoid(gate))).astype(BF16)
            upd = upd + jnp.dot(jnp.where(in_head_r, ktw, 0.0).astype(BF16), vh, preferred_element_type=F32)
        s_scr[g] = gc_ref[g] * s_prev + upd

    @pl.when(n == pl.num_programs(0) - 1)
    def _():
        sfin_ref[...] = s_scr[...]


def _retention(q, k, v, gb, s0_pairs, consts):
    t = q.shape[0]
    c = RET_CHUNK
    decay, q_w, k_w_t, g_c, _ = consts
    kt = k.T
    full = lambda a: pl.BlockSpec(a.shape, lambda n: (0,) * a.ndim)
    return pl.pallas_call(
        _ret_body,
        grid=(t // c,),
        in_specs=[
            pl.BlockSpec((c, W_BQK), lambda n: (n, 0)),
            pl.BlockSpec((c, W_BQK), lambda n: (n, 0)),
            pl.BlockSpec((W_BQK, c), lambda n: (0, n)),
            pl.BlockSpec((c, W_BV), lambda n: (n, 0)),
            pl.BlockSpec((c, W_BV), lambda n: (n, 0)),
            full(s0_pairs), full(decay), full(q_w), full(k_w_t), full(g_c),
        ],
        out_specs=[pl.BlockSpec((c, W_BV), lambda n: (n, 0)), full(s0_pairs)],
        out_shape=[jax.ShapeDtypeStruct((t, W_BV), BF16), jax.ShapeDtypeStruct(s0_pairs.shape, F32)],
        scratch_shapes=[pltpu.VMEM(s0_pairs.shape, F32)],
        compiler_params=_cparams(("arbitrary",)),
        name="retention",
    )(q, k, kt, v, gb, s0_pairs, decay, q_w, k_w_t, g_c)


def _kmean_body(k_ref, o_ref):
    nb = o_ref.shape[0]
    kk = k_ref[...].reshape(nb, MOBA_BLOCK, k_ref.shape[1])
    o_ref[...] = jnp.sum(kk, axis=1) * (1.0 / MOBA_BLOCK)


def _kmean(k):
    t, w = k.shape
    nblk = t // MOBA_BLOCK
    nb = 8
    assert nblk % nb == 0
    return pl.pallas_call(
        _kmean_body,
        grid=(nblk // nb,),
        in_specs=[pl.BlockSpec((nb * MOBA_BLOCK, w), lambda i: (i, 0))],
        out_specs=pl.BlockSpec((nb, w), lambda i: (i, 0)),
        out_shape=jax.ShapeDtypeStruct((nblk, w), F32),
        compiler_params=_cparams(("parallel",)),
        name="moba_kmean",
    )(k)


def _top_blocks(gate, valid):
    lane = lax.broadcasted_iota(jnp.int32, gate.shape, 1).astype(F32)
    g = jnp.where(valid, gate, -jnp.inf)
    picked = jnp.zeros(gate.shape, F32)
    idxs = []
    for _ in range(MOBA_TOPK):
        m = jnp.max(g, axis=-1, keepdims=True)
        cand = jnp.logical_and(g == m, m > -jnp.inf)
        idx = jnp.min(jnp.where(cand, lane, float(LANES)), axis=-1, keepdims=True)
        pick = lane == idx
        picked = jnp.where(pick, 1.0, picked)
        g = jnp.where(pick, -jnp.inf, g)
        idxs.append(idx)
    return picked, idxs


def _select_body(q_ref, km_ref, o_ref, *, pos0):
    tq = q_ref.shape[0]
    q = q_ref[...]
    km = km_ref[...]
    pw = 2 * HEAD_DIM
    lane_q = lax.broadcasted_iota(jnp.int32, q.shape, 1)
    blk = lax.broadcasted_iota(jnp.int32, (tq, LANES), 1)
    qpos = pos0 + pl.program_id(0) * tq + lax.broadcasted_iota(jnp.int32, (tq, LANES), 0)
    past = blk < qpos // MOBA_BLOCK
    for h in range(H_C):
        g, hh = h // 2, h % 2
        qh = jnp.where(lane_q // HEAD_DIM == h, q, 0.0)
        gate = _dot_nt(qh, km, precision=lax.Precision.HIGHEST)
        picked, _ = _top_blocks(gate, past)
        sel = jnp.where(jnp.logical_and(past, picked == 0.0), NEG, 0.0)
        if hh == 0:
            sel = pltpu.roll(sel, HEAD_DIM, 1)
        in_head = (blk < HEAD_DIM) if hh == 0 else (blk >= HEAD_DIM)
        qa = jnp.where(in_head, q[:, g * pw:(g + 1) * pw] * QK_SCALE, sel)
        o_ref[:, h * LANES:(h + 1) * LANES] = qa.astype(BF16)


def _select(q, kmean_pad, pos0):
    t, w = q.shape
    tq = min(ROW_TILE, t)
    assert (pos0 + t) // MOBA_BLOCK <= HEAD_DIM
    return pl.pallas_call(
        functools.partial(_select_body, pos0=pos0),
        grid=(t // tq,),
        in_specs=[pl.BlockSpec((tq, w), lambda i: (i, 0)),
                  pl.BlockSpec(kmean_pad.shape, lambda i: (0, 0))],
        out_specs=pl.BlockSpec((tq, H_C * LANES), lambda i: (i, 0)),
        out_shape=jax.ShapeDtypeStruct((t, H_C * LANES), BF16),
        compiler_params=_cparams(("parallel",)),
        name="moba_select",
    )(q, kmean_pad)


def _attn_c_body(qi_ref, kj_ref, kind_ref, fl_ref, q_ref, k_ref, v_ref, b_ref, o_ref, m_scr, acc_scr):
    t = pl.program_id(0)
    flag = fl_ref[t]
    j = kj_ref[t]

    @pl.when((flag & 1) != 0)
    def _():
        m_scr[...] = jnp.full_like(m_scr, NEG)
        acc_scr[...] = jnp.zeros_like(acc_scr)

    pw = 2 * HEAD_DIM
    tk = k_ref.shape[0]
    reps = tk // LANES
    lane_k = lax.broadcasted_iota(jnp.int32, (tk, pw), 1)
    for h in range(H_C):
        g, hh = h // 2, h % 2
        kp = k_ref[:, g * pw:(g + 1) * pw]
        vp = v_ref[:, g * pw:(g + 1) * pw]
        in_head = (lane_k < HEAD_DIM) if hh == 0 else (lane_k >= HEAD_DIM)
        onehot = lane_k == j + (HEAD_DIM if hh == 0 else 0)
        ka = jnp.where(in_head, kp, jnp.where(onehot, 1.0, 0.0).astype(BF16))
        va = jnp.where(in_head, vp, jnp.ones_like(vp))
        s = _dot_nt(q_ref[:, h * LANES:(h + 1) * LANES], ka) + b_ref[h]
        m_prev = m_scr[h]
        m_new = jnp.maximum(m_prev, jnp.max(s, axis=-1, keepdims=True))
        alpha = jnp.exp(m_prev - m_new)
        p = jnp.exp(s - jnp.concatenate([m_new] * reps, axis=1))
        acc_scr[h] = alpha * acc_scr[h] + jnp.dot(p.astype(BF16), va, preferred_element_type=F32)
        m_scr[h] = m_new

    @pl.when((flag & 2) != 0)
    def _():
        lane = lax.broadcasted_iota(jnp.int32, (q_ref.shape[0], pw), 1)
        for g in range(H_C // 2):
            a0, a1 = acc_scr[2 * g], acc_scr[2 * g + 1]
            o0 = a0 / pltpu.roll(a0, HEAD_DIM, 1)
            o1 = a1 / pltpu.roll(a1, HEAD_DIM, 1)
            o_ref[:, g * pw:(g + 1) * pw] = jnp.where(lane < HEAD_DIM, o0, o1).astype(BF16)


def _attn_c(q_aug, k, v, bias, sched):
    t = q_aug.shape[0]
    tq = min(ATT_C_TQ, t)
    tk = MOBA_BLOCK
    qi, kj, kind, flags, _ = sched
    n = qi.shape[0]
    return pl.pallas_call(
        _attn_c_body,
        grid_spec=pltpu.PrefetchScalarGridSpec(
            num_scalar_prefetch=4,
            grid=(n,),
            in_specs=[
                pl.BlockSpec((tq, H_C * LANES), lambda s, qi, kj, kd, fl: (qi[s], 0)),
                pl.BlockSpec((tk, W_C), lambda s, qi, kj, kd, fl: (kj[s], 0)),
                pl.BlockSpec((tk, W_C), lambda s, qi, kj, kd, fl: (kj[s], 0)),
                pl.BlockSpec((None, H_C, tq, tk), lambda s, qi, kj, kd, fl: (kd[s], 0, 0, 0)),
            ],
            out_specs=pl.BlockSpec((tq, W_C), lambda s, qi, kj, kd, fl: (qi[s], 0)),
            scratch_shapes=[pltpu.VMEM((H_C, tq, LANES), F32), pltpu.VMEM((H_C, tq, 2 * HEAD_DIM), F32)],
        ),
        out_shape=jax.ShapeDtypeStruct((t, W_C), BF16),
        compiler_params=_cparams(("arbitrary",)),
        name="attn_c",
    )(qi, kj, kind, flags, q_aug, k, v, bias)


def _merge_body(x_ref, g_ref, wg_ref, oa_ref, ob_ref, oc_ref, wa_ref, wb_ref, wc_ref, wo_ref, o_ref):
    x = x_ref[...]
    h = _rms_rows(x, g_ref[...]).astype(BF16)
    mix = None
    for n, (o_in, w_up) in enumerate(((oa_ref, wa_ref), (ob_ref, wb_ref), (oc_ref, wc_ref))):
        y = jnp.dot(o_in[...], w_up[...], preferred_element_type=F32)
        gl = jnp.dot(h, wg_ref[:, n * D_MODEL:(n + 1) * D_MODEL], preferred_element_type=F32)
        term = jax.nn.sigmoid(gl) * y
        mix = term if mix is None else mix + term
    o_ref[...] = x + jnp.dot(mix.astype(BF16), wo_ref[...], preferred_element_type=F32)


def _merge(x, g, wg, oa, ob, oc, wa, wb, wc, wo):
    m, d = x.shape
    tm = min(ROW_TILE, m)
    rows = lambda w: pl.BlockSpec((tm, w), lambda i: (i, 0))
    full = lambda a: pl.BlockSpec(a.shape, lambda i: (0, 0))
    return pl.pallas_call(
        _merge_body,
        grid=(m // tm,),
        in_specs=[rows(d), full(g), full(wg), rows(W_A), rows(W_BV), rows(W_C),
                  full(wa), full(wb), full(wc), full(wo)],
        out_specs=rows(d),
        out_shape=jax.ShapeDtypeStruct((m, d), F32),
        compiler_params=_cparams(("parallel",)),
        name="merge",
    )(x, g, wg, oa, ob, oc, wa, wb, wc, wo)


DEC_ROWS = 16


def _dec_a_body(pt_ref, q_ref, *refs, npg, lam_init):
    k_refs, v_refs = refs[:npg], refs[npg:2 * npg]
    b_ref, knew_ref, vnew_ref, b0_ref, lam_ref, sg_ref, o_ref, m_scr, l_scr, acc_scr = refs[2 * npg:]
    p = pl.program_id(1)
    prow = k_refs[0].shape[0]

    @pl.when(p == 0)
    def _():
        m_scr[...] = jnp.full_like(m_scr, NEG)
        l_scr[...] = jnp.zeros_like(l_scr)
        acc_scr[...] = jnp.zeros_like(acc_scr)

    qrows = q_ref[...].astype(BF16)
    s = jnp.concatenate([_dot_nt(qrows, k_refs[i][...].astype(BF16)) for i in range(npg)], axis=1) + b_ref[...]
    m_prev = m_scr[...]
    m_new = jnp.maximum(m_prev, jnp.max(s, axis=-1, keepdims=True))
    alpha = jnp.exp(m_prev - m_new)
    pr = jnp.exp(s - m_new).astype(BF16)
    l_scr[...] = alpha * l_scr[...] + jnp.sum(pr.astype(F32), axis=-1, keepdims=True)
    acc = alpha * acc_scr[...]
    for i in range(npg):
        acc = acc + jnp.dot(pr[:, i * prow:(i + 1) * prow], v_refs[i][...].astype(BF16),
                            preferred_element_type=F32)
    acc_scr[...] = acc
    m_scr[...] = m_new

    @pl.when(p == pl.num_programs(1) - 1)
    def _():
        k_new = knew_ref[...].astype(BF16).astype(F32)
        v_new = vnew_ref[...].astype(BF16).astype(F32)
        s_new = jnp.sum(qrows.astype(F32) * k_new, axis=-1, keepdims=True) + b0_ref[...]
        m_fin = jnp.maximum(m_scr[...], s_new)
        a_fin = jnp.exp(m_scr[...] - m_fin)
        p_new = jnp.exp(s_new - m_fin).astype(BF16).astype(F32)
        l_fin = a_fin * l_scr[...] + p_new
        o = (a_fin * acc_scr[...] + p_new * v_new) / l_fin
        lam = _lambda_value(lam_ref, lam_init)
        hw = 2 * HEAD_DIM
        for h in range(H_A):
            oh = o[2 * h:2 * h + 1, :] - lam * o[2 * h + 1:2 * h + 2, :]
            o_ref[:, h * hw:(h + 1) * hw] = _rms_rows(oh, sg_ref[...]) * (1.0 - lam_init)


def _dec_a(layer, page_table, qrows, cache_k, cache_v, bias, k_new, v_new, b0, lam_vecs, subln_g, lam_init):
    bsz, n_pages = page_table.shape
    prow = cache_k.shape[2]
    npg = min(PAGES_PER_STEP, n_pages)
    assert n_pages % npg == 0
    n_steps = n_pages // npg
    hw = 2 * HEAD_DIM
    seq_rows = pl.BlockSpec((None, DEC_ROWS, hw), lambda b, p, pt: (b, 0, 0))
    const = lambda a: pl.BlockSpec(a.shape, lambda b, p, pt: (0,) * a.ndim)

    def page_spec(i):
        return pl.BlockSpec((None, None, prow, hw), lambda b, p, pt: (layer, pt[b, p * npg + i], 0, 0))

    in_specs = ([seq_rows] + [page_spec(i) for i in range(npg)] + [page_spec(i) for i in range(npg)]
                + [pl.BlockSpec((None, DEC_ROWS, npg * prow), lambda b, p, pt: (p // (n_steps - 1) if n_steps > 1 else 1, 0, 0)),
                   seq_rows, seq_rows, const(b0), const(lam_vecs), const(subln_g)])
    return pl.pallas_call(
        functools.partial(_dec_a_body, npg=npg, lam_init=lam_init),
        grid_spec=pltpu.PrefetchScalarGridSpec(
            num_scalar_prefetch=1,
            grid=(bsz, n_steps),
            in_specs=in_specs,
            out_specs=pl.BlockSpec((None, 1, W_A), lambda b, p, pt: (b, 0, 0)),
            scratch_shapes=[pltpu.VMEM((DEC_ROWS, 1), F32), pltpu.VMEM((DEC_ROWS, 1), F32),
                            pltpu.VMEM((DEC_ROWS, hw), F32)],
        ),
        out_shape=jax.ShapeDtypeStruct((bsz, 1, W_A), F32),
        compiler_params=_cparams(("parallel", "arbitrary")),
        name="decode_attn_a",
    )(page_table, qrows, *([cache_k] * npg), *([cache_v] * npg), bias, k_new, v_new, b0, lam_vecs, subln_g)


def _dec_ret_body(qc_ref, kc_ref, v_ref, gb_ref, s0_ref, o_ref, s_ref, *, gammas):
    for h in range(H_B):
        vh = v_ref[:, h * DV_B:(h + 1) * DV_B]
        s_new = float(gammas[h]) * s0_ref[h] + kc_ref[h] * vh
        s_ref[h] = s_new
        o = jnp.sum(qc_ref[h] * s_new, axis=0, keepdims=True)
        o = o * lax.rsqrt(jnp.mean(o * o, axis=-1, keepdims=True) + NORM_EPS)
        gate = gb_ref[:, h * DV_B:(h + 1) * DV_B]
        o_ref[:, h * DV_B:(h + 1) * DV_B] = o * (gate * jax.nn.sigmoid(gate))


def _dec_ret(layer, q, k, v, gb, state, gammas):
    bsz = q.shape[0]
    qc = q.reshape(bsz, H_B, DK_B, 1)
    kc = k.reshape(bsz, H_B, DK_B, 1)
    col = pl.BlockSpec((None, H_B, DK_B, 1), lambda b: (b, 0, 0, 0))
    row = pl.BlockSpec((None, 1, W_BV), lambda b: (b, 0, 0))
    return pl.pallas_call(
        functools.partial(_dec_ret_body, gammas=gammas),
        grid=(bsz,),
        in_specs=[col, col, row, row,
                  pl.BlockSpec((None, None, H_B, DK_B, DV_B), lambda b: (layer, b, 0, 0, 0))],
        out_specs=[row, pl.BlockSpec((None, H_B, DK_B, DV_B), lambda b: (b, 0, 0, 0))],
        out_shape=[jax.ShapeDtypeStruct((bsz, 1, W_BV), F32),
                   jax.ShapeDtypeStruct((bsz, H_B, DK_B, DV_B), F32)],
        compiler_params=_cparams(("parallel",)),
        name="decode_retention",
    )(qc, kc, v.reshape(bsz, 1, W_BV), gb.reshape(bsz, 1, W_BV), state)


def _dec_c1_body(pt_ref, q_ref, *refs, npg, ppb, n_past):
    k_refs = refs[:npg]
    o_ref, km_scr = refs[npg:]
    p = pl.program_id(1)

    @pl.when(p == 0)
    def _():
        km_scr[...] = jnp.zeros_like(km_scr)

    blk_lane = lax.broadcasted_iota(jnp.int32, km_scr.shape, 1)
    km = km_scr[...]
    for i in range(npg // ppb):
        tot = None
        for u in range(ppb):
            part = jnp.sum(k_refs[i * ppb + u][...], axis=-1, keepdims=True)
            tot = part if tot is None else tot + part
        km = jnp.where(blk_lane == p * (npg // ppb) + i, tot * (1.0 / MOBA_BLOCK), km)
    km_scr[...] = km

    @pl.when(p == pl.num_programs(1) - 1)
    def _():
        prod = km_scr[...] * q_ref[...]
        gate = jnp.sum(prod.reshape(H_C, HEAD_DIM, LANES), axis=1)
        blk = lax.broadcasted_iota(jnp.int32, gate.shape, 1)
        _, idxs = _top_blocks(gate, blk < n_past)
        out = jnp.full(gate.shape, -1, jnp.int32)
        for r, idx in enumerate(idxs):
            out = jnp.where(jnp.logical_and(blk == r, idx < float(LANES)), idx.astype(jnp.int32), out)
        o_ref[...] = out


def _dec_c1(layer, page_table, qcol, cache_kt):
    bsz, n_pages = page_table.shape
    page = cache_kt.shape[3]
    ppb = MOBA_BLOCK // page
    npg = min(PAGES_PER_STEP, n_pages)
    assert MOBA_BLOCK % page == 0 and npg % ppb == 0 and n_pages % npg == 0
    n_past = n_pages // ppb
    assert n_past <= LANES

    def page_spec(i):
        return pl.BlockSpec((None, None, W_C, page), lambda b, p, pt: (layer, pt[b, p * npg + i], 0, 0))

    return pl.pallas_call(
        functools.partial(_dec_c1_body, npg=npg, ppb=ppb, n_past=n_past),
        grid_spec=pltpu.PrefetchScalarGridSpec(
            num_scalar_prefetch=1,
            grid=(bsz, n_pages // npg),
            in_specs=[pl.BlockSpec((None, W_C, 1), lambda b, p, pt: (b, 0, 0))] + [page_spec(i) for i in range(npg)],
            out_specs=pl.BlockSpec((None, H_C, LANES), lambda b, p, pt: (b, 0, 0)),
            scratch_shapes=[pltpu.VMEM((W_C, LANES), F32)],
        ),
        out_shape=jax.ShapeDtypeStruct((bsz, H_C, LANES), jnp.int32),
        compiler_params=_cparams(("parallel", "arbitrary")),
        name="decode_moba_select",
    )(page_table, qcol, *([cache_kt] * npg))


def _dec_c2_body(pt_ref, idx_ref, q_ref, *refs, ppb):
    nk = MOBA_TOPK * ppb
    k_refs, v_refs, b_refs = refs[:nk], refs[nk:2 * nk], refs[2 * nk:2 * nk + MOBA_TOPK]
    knew_ref, vnew_ref, b0_ref, o_ref = refs[2 * nk + MOBA_TOPK:]
    b, h = pl.program_id(0), pl.program_id(1)
    q = q_ref[...]
    q_rows = jnp.broadcast_to(q, (DEC_ROWS, HEAD_DIM)).astype(BF16)
    ss = []
    for r in range(MOBA_TOPK):
        valid = idx_ref[(b * H_C + h) * MOBA_TOPK + r] >= 0
        s = jnp.concatenate([jnp.dot(q_rows, k_refs[r * ppb + u][...].astype(BF16), preferred_element_type=F32)
                             for u in range(ppb)], axis=1) + b_refs[r][...]
        ss.append(jnp.where(valid, s, NEG))
    k_new = knew_ref[...].astype(BF16).astype(F32)
    v_new = vnew_ref[...].astype(BF16).astype(F32)
    s_new = jnp.sum(q.astype(BF16).astype(F32) * k_new, axis=-1, keepdims=True) + b0_ref[...][:, 0:1]
    m = s_new
    for s in ss:
        m = jnp.maximum(m, jnp.max(s, axis=-1, keepdims=True))
    p_new = jnp.exp(s_new - m).astype(BF16).astype(F32)
    l = p_new
    o = p_new * v_new
    page = k_refs[0].shape[1]
    for r, s in enumerate(ss):
        p = jnp.exp(s - m).astype(BF16)
        l = l + jnp.sum(p.astype(F32), axis=-1, keepdims=True)
        for u in range(ppb):
            o = o + _dot_nt(p[:, u * page:(u + 1) * page], v_refs[r * ppb + u][...].astype(BF16))
    o_ref[...] = (o / l)[0:1, :]


def _dec_c2(layer, page_table, idx, q, cache_kt, cache_vt, bias_blocks, k_new, v_new, b0):
    bsz, n_pages = page_table.shape
    page = cache_kt.shape[3]
    ppb = MOBA_BLOCK // page
    idx_flat = idx[:, :, :MOBA_TOPK].reshape(-1)

    def blk_of(b, h, r, ix):
        return jnp.maximum(ix[(b * H_C + h) * MOBA_TOPK + r], 0)

    def page_spec(r, u):
        return pl.BlockSpec((None, None, HEAD_DIM, page),
                            lambda b, h, pt, ix: (layer, pt[b, blk_of(b, h, r, ix) * ppb + u], h, 0))

    def bias_spec(r):
        return pl.BlockSpec((None, None, 1, MOBA_BLOCK), lambda b, h, pt, ix: (blk_of(b, h, r, ix), h, 0, 0))

    seq_head = pl.BlockSpec((None, None, 1, HEAD_DIM), lambda b, h, pt, ix: (b, h, 0, 0))
    pages = [page_spec(r, u) for r in range(MOBA_TOPK) for u in range(ppb)]
    return pl.pallas_call(
        functools.partial(_dec_c2_body, ppb=ppb),
        grid_spec=pltpu.PrefetchScalarGridSpec(
            num_scalar_prefetch=2,
            grid=(bsz, H_C),
            in_specs=([seq_head] + pages + pages + [bias_spec(r) for r in range(MOBA_TOPK)]
                      + [seq_head, seq_head, pl.BlockSpec((None, 1, HEAD_DIM), lambda b, h, pt, ix: (h, 0, 0))]),
            out_specs=seq_head,
        ),
        out_shape=jax.ShapeDtypeStruct((bsz, H_C, 1, HEAD_DIM), F32),
        compiler_params=_cparams(("parallel", "arbitrary")),
        name="decode_moba_attn",
    )(page_table, idx_flat, q, *([cache_kt] * (MOBA_TOPK * ppb)), *([cache_vt] * (MOBA_TOPK * ppb)),
      *([bias_blocks] * MOBA_TOPK), k_new, v_new, b0)


def _rotary_tables(pos):
    half = HEAD_DIM // 2
    inv = 1.0 / (10000.0 ** (jnp.arange(half, dtype=F32) / half))
    ang = pos.astype(F32)[:, None] * inv[None, :]
    cos, sin = jnp.cos(ang), jnp.sin(ang)
    cos_t = jnp.tile(jnp.concatenate([cos, cos], axis=1), (1, H_B))
    sin_t = jnp.tile(jnp.concatenate([-sin, sin], axis=1), (1, H_B))
    return cos_t, sin_t


def _block_diag_ones():
    i = np.arange(W_A) // HEAD_DIM
    return jnp.asarray((i[:, None] == i[None, :]).astype(np.float32), dtype=BF16)


SEG_A = ((W_A, "norm", 1.0, QK_SCALE, False, True),
         (W_A, "norm", 1.0, 1.0, True, True),
         (W_A, "plain", 1.0, 1.0, True, True))
SEG_B = ((W_BQK, "rot", 1.0, 1.0, True, False),
         (W_BQK, "rot", DK_B ** -0.5, 1.0, True, False),
         (W_BV, "plain", 1.0, 1.0, True, False),
         (W_BV, "plain", 1.0, 1.0, True, False))
SEG_C = ((W_C, "norm", 1.0, QK_SCALE, True, True),
         (W_C, "norm", 1.0, 1.0, True, True),
         (W_C, "plain", 1.0, 1.0, True, True))
SEG_C_PROMPT = ((W_C, "norm", 1.0, 1.0, True, False),) + SEG_C[1:]


def _layer_weights(l, w):
    bf = lambda a: a.astype(BF16)
    o_b, o_c, o_g = 3 * W_A, 3 * W_A + 2 * W_BQK + 2 * W_BV, 3 * W_A + 2 * W_BQK + 2 * W_BV + 3 * W_C
    w_in = w["w_in"][l]
    tile = lambda g: jnp.tile(g, W_A // HEAD_DIM)[None, :]
    return dict(
        ffn1=(w["ln_ffn1_g"][l][None, :], bf(w["ffn1_w_gate"][l]), bf(w["ffn1_w_up"][l]), bf(w["ffn1_w_down"][l])),
        ffn2=(w["ln_ffn2_g"][l][None, :], bf(w["ffn2_w_gate"][l]), bf(w["ffn2_w_up"][l]), bf(w["ffn2_w_down"][l])),
        ln_mix=w["ln_mix_g"][l][None, :],
        w_a=bf(w_in[:, :o_b]), w_b=bf(w_in[:, o_b:o_c]), w_c=bf(w_in[:, o_c:o_g]), w_g=bf(w_in[:, o_g:]),
        gains_a=jnp.concatenate([tile(w["qn_a_g"][l]), tile(w["kn_a_g"][l])], axis=0),
        gains_c=jnp.concatenate([tile(w["qn_c_g"][l]), tile(w["kn_c_g"][l])], axis=0),
        lam_vecs=jnp.stack([w["lam_q1"][l], w["lam_k1"][l], w["lam_q2"][l], w["lam_k2"][l]]),
        subln=w["subln_a_g"][l][None, :],
        w_up_a=bf(w["w_up_a"][l]), w_up_b=bf(w["w_up_b"][l]), w_up_c=bf(w["w_up_c"][l]), w_o=bf(w["w_o"][l]),
    )


def kernel(x_prompt, x_sample, cache_a_k, cache_a_v, cache_c_k, cache_c_v, state_ret, page_table, ln_ffn1_g, ffn1_w_gate, ffn1_w_up, ffn1_w_down, ln_mix_g, w_in, qn_a_g, kn_a_g, lam_q1, lam_k1, lam_q2, lam_k2, subln_a_g, qn_c_g, kn_c_g, rel_bias, w_up_a, w_up_b, w_up_c, w_o, ln_ffn2_g, ffn2_w_gate, ffn2_w_up, ffn2_w_down):
    w = dict(ln_ffn1_g=ln_ffn1_g, ffn1_w_gate=ffn1_w_gate, ffn1_w_up=ffn1_w_up, ffn1_w_down=ffn1_w_down,
             ln_mix_g=ln_mix_g, w_in=w_in, qn_a_g=qn_a_g, kn_a_g=kn_a_g, lam_q1=lam_q1, lam_k1=lam_k1,
             lam_q2=lam_q2, lam_k2=lam_k2, subln_a_g=subln_a_g, qn_c_g=qn_c_g, kn_c_g=kn_c_g,
             w_up_a=w_up_a, w_up_b=w_up_b, w_up_c=w_up_c, w_o=w_o, ln_ffn2_g=ln_ffn2_g,
             ffn2_w_gate=ffn2_w_gate, ffn2_w_up=ffn2_w_up, ffn2_w_down=ffn2_w_down)
    depth = w_in.shape[0]
    assert x_prompt.shape[0] == 1 and x_sample.shape[1] == 1
    t = x_prompt.shape[1]
    bsz, n_pages = page_table.shape
    page = cache_a_k.shape[2]
    n_pool = cache_a_k.shape[1]
    past = n_pages * page
    assert t % MOBA_BLOCK == 0 and past % MOBA_BLOCK == 0 and t % RET_CHUNK == 0

    bd = _block_diag_ones()
    cos_p, sin_p = _rotary_tables(jnp.arange(t, dtype=jnp.int32))
    cos_s, sin_s = _rotary_tables(jnp.full((bsz,), past, jnp.int32))
    ret_consts = _ret_constants(RET_CHUNK)
    gammas = ret_consts[4]

    ta = min(ATT_A_TILE, t)
    sched_a = _tile_schedule(t, ta, ta)
    bias_a = _bias_tiles(rel_bias, sched_a[4], H_A, 0, ta, ta, 1)
    tcq = min(ATT_C_TQ, t)
    sched_c = _tile_schedule(t, tcq, MOBA_BLOCK)
    bias_c = _bias_tiles(rel_bias, sched_c[4], H_C, H_A, tcq, MOBA_BLOCK, 1)

    npg = min(PAGES_PER_STEP, n_pages)
    dec_steps = n_pages // npg
    assert npg * page >= T5_FAR
    bias_da = _bias_tiles(rel_bias, [1 << 24, past - (dec_steps - 1) * npg * page], H_A, 0, 8, npg * page, 0)
    bias_da = jnp.repeat(bias_da[:, :, 0, :], 2, axis=1)
    same_head = (jnp.arange(2 * H_A) // 2)[:, None, None] == jnp.arange(H_A)[None, None, :]
    bias_da = jnp.where(same_head[None], bias_da[..., None], NEG).reshape(2, 2 * H_A, npg * page * H_A)
    bias_da = jnp.concatenate([bias_da, jnp.zeros_like(bias_da)], axis=1)
    n_past_blk = past // MOBA_BLOCK
    bias_dc = _bias_tiles(rel_bias, [past - j * MOBA_BLOCK for j in range(n_past_blk)], H_C, H_A, 8, MOBA_BLOCK, 0)
    bias_dc = bias_dc[:, :, 0:1, :]
    b_self = rel_bias[0] - rel_bias[N_BUCKETS - 1]
    b0_a = jnp.concatenate([jnp.repeat(b_self[:H_A], 2), jnp.zeros((8,), F32)])[:, None]
    b0_c = jnp.broadcast_to(b_self[H_A:, None, None], (H_C, 1, HEAD_DIM))

    ck_a = cache_a_k.reshape(depth, n_pool, page * H_A, 2 * HEAD_DIM)
    cv_a = cache_a_v.reshape(depth, n_pool, page * H_A, 2 * HEAD_DIM)
    ck_c = jnp.transpose(cache_c_k, (0, 1, 3, 4, 2)).reshape(depth, n_pool, W_C, page)
    cv_c = jnp.transpose(cache_c_v, (0, 1, 3, 4, 2)).reshape(depth, n_pool, W_C, page)
    comp_mask = jnp.eye(2, dtype=F32)[None, None, :, :, None]

    xp = x_prompt[0]
    xs = x_sample[:, 0]
    rows_p, rows_s = [], []
    for l in range(depth):
        lw = _layer_weights(l, w)
        lam_init = 0.8 - 0.6 * math.exp(-0.3 * l)

        xp = _ffn(xp, *lw["ffn1"])
        qa_bf, ka, ka_bf, va, va_bf = _proj(xp, lw["ln_mix"], lw["w_a"], SEG_A, gains=lw["gains_a"], bd=bd)
        qb, kb, vb, gb = _proj(xp, lw["ln_mix"], lw["w_b"], SEG_B, cos=cos_p, sin=sin_p)
        qc, kc, kc_bf, vc, vc_bf = _proj(xp, lw["ln_mix"], lw["w_c"], SEG_C_PROMPT, gains=lw["gains_c"], bd=bd)
        oa = _attn_a(qa_bf, ka_bf, va_bf, bias_a, sched_a, lw["lam_vecs"], lw["subln"], lam_init)
        s0 = jnp.zeros((H_B // 2, 2 * DK_B, DV_B), F32)
        ob, s_fin = _retention(qb, kb, vb, gb, s0, ret_consts)
        kmean = _kmean(kc)
        kmean_pad = jnp.pad(kmean, ((0, LANES - kmean.shape[0]), (0, 0)))
        qc_aug = _select(qc, kmean_pad, 0)
        oc = _attn_c(qc_aug, kc_bf, vc_bf, bias_c, sched_c)
        xp = _merge(xp, lw["ln_mix"], lw["w_g"], oa, ob, oc, lw["w_up_a"], lw["w_up_b"], lw["w_up_c"], lw["w_o"])
        xp = _ffn(xp, *lw["ffn2"])
        rows_p.append((ka.reshape(1, t, H_A, 2 * HEAD_DIM), va.reshape(1, t, H_A, 2 * HEAD_DIM),
                       kc.reshape(1, t, H_C, HEAD_DIM), vc.reshape(1, t, H_C, HEAD_DIM),
                       s_fin.reshape(1, H_B, DK_B, DV_B)))

        xs = _ffn(xs, *lw["ffn1"])
        sqa_bf, ska, _, sva, _ = _proj(xs, lw["ln_mix"], lw["w_a"], SEG_A, gains=lw["gains_a"], bd=bd)
        sqb, skb, svb, sgb = _proj(xs, lw["ln_mix"], lw["w_b"], SEG_B, cos=cos_s, sin=sin_s)
        sqc, sqc_bf, skc, _, svc, _ = _proj(xs, lw["ln_mix"], lw["w_c"], SEG_C, gains=lw["gains_c"], bd=bd)
        pad_rows = lambda a: jnp.pad(a, ((0, 0), (0, DEC_ROWS - 2 * H_A), (0, 0)))
        q_rows = (sqa_bf.astype(F32).reshape(bsz, H_A, 1, 2, HEAD_DIM) * comp_mask).reshape(bsz, 2 * H_A, 2 * HEAD_DIM)
        per_head = lambda a: jnp.repeat(a.reshape(bsz, H_A, 2 * HEAD_DIM), 2, axis=1)
        soa = _dec_a(l, page_table, pad_rows(q_rows), ck_a, cv_a, bias_da, pad_rows(per_head(ska)),
                     pad_rows(per_head(sva)), b0_a, lw["lam_vecs"], lw["subln"], lam_init)
        sob, s_new = _dec_ret(l, sqb, skb, svb, sgb, state_ret, gammas)
        idx = _dec_c1(l, page_table, sqc.reshape(bsz, W_C, 1), ck_c)
        h4 = lambda a: a.reshape(bsz, H_C, 1, HEAD_DIM)
        soc = _dec_c2(l, page_table, idx, h4(sqc_bf.astype(F32)), ck_c, cv_c, bias_dc, h4(skc), h4(svc), b0_c)
        xs = _merge(xs, lw["ln_mix"], lw["w_g"], soa.reshape(bsz, W_A).astype(BF16),
                    sob.reshape(bsz, W_BV).astype(BF16), soc.reshape(bsz, W_C).astype(BF16),
                    lw["w_up_a"], lw["w_up_b"], lw["w_up_c"], lw["w_o"])
        xs = _ffn(xs, *lw["ffn2"])
        rows_s.append((ska.reshape(bsz, 1, H_A, 2 * HEAD_DIM), sva.reshape(bsz, 1, H_A, 2 * HEAD_DIM),
                       skc.reshape(bsz, 1, H_C, HEAD_DIM), svc.reshape(bsz, 1, H_C, HEAD_DIM), s_new))

    stack = lambda rows, i: jnp.stack([r[i] for r in rows])
    return (xp[None], xs[:, None, :],
            stack(rows_p, 0), stack(rows_p, 1), stack(rows_p, 2), stack(rows_p, 3), stack(rows_p, 4),
            stack(rows_s, 0), stack(rows_s, 1), stack(rows_s, 2), stack(rows_s, 3), stack(rows_s, 4))
```

```python
import functools
import math

import jax
import jax.numpy as jnp
import numpy as np
from jax import lax
from jax.experimental import pallas as pl
from jax.experimental.pallas import tpu as pltpu

F32 = jnp.float32
BF16 = jnp.bfloat16

D_MODEL = 1024
D_FF = 2816
HEAD_DIM = 64
H_A = 4
H_B = 4
DK_B = 64
DV_B = 128
H_C = 8
RET_CHUNK = 128
MOBA_BLOCK = 256
MOBA_TOPK = 3
N_BUCKETS = 32
MAX_DISTANCE = 128
NORM_EPS = 1e-6
NEG = -1e30
W_A = H_A * 2 * HEAD_DIM
W_BQK = H_B * DK_B
W_BV = H_B * DV_B
W_C = H_C * HEAD_DIM
QK_SCALE = HEAD_DIM ** -0.5
LOG2E = math.log2(math.e)

LANES = 128
VMEM_LIMIT = 56 * 1024 * 1024
ROW_TILE = 512
ATT_A_TILE = 512
ATT_C_TQ = 512
PAGES_PER_STEP = 16


def _cparams(sem):
    return pltpu.CompilerParams(dimension_semantics=sem, vmem_limit_bytes=VMEM_LIMIT)


def _rms_rows(x, g):
    ms = jnp.mean(x * x, axis=-1, keepdims=True)
    return x * lax.rsqrt(ms + NORM_EPS) * g


def _dot_nt(a, b, **kw):
    return lax.dot_general(a, b, (((1,), (1,)), ((), ())), preferred_element_type=F32, **kw)


def _t5_thresholds():
    n = np.arange(0, 4 * MAX_DISTANCE)
    max_exact = N_BUCKETS // 2
    nf = np.maximum(n, 1).astype(np.float32)
    large = max_exact + (np.log(nf / np.float32(max_exact)) / np.float32(math.log(MAX_DISTANCE / max_exact))
                         * np.float32(N_BUCKETS - max_exact)).astype(np.int32)
    bucket = np.where(n < max_exact, n, np.minimum(large, N_BUCKETS - 1))
    return [int(np.argmax(bucket >= t)) for t in range(N_BUCKETS)]


T5_THR = _t5_thresholds()
T5_FAR = T5_THR[-1]


def _ffn_body(x_ref, g_ref, wg_ref, wu_ref, wd_ref, o_ref, h_scr, acc_scr):
    j = pl.program_id(1)

    @pl.when(j == 0)
    def _():
        h_scr[...] = _rms_rows(x_ref[...], g_ref[...]).astype(BF16)
        acc_scr[...] = jnp.zeros_like(acc_scr)

    h = h_scr[...]
    a = jnp.dot(h, wg_ref[...], preferred_element_type=F32)
    b = jnp.dot(h, wu_ref[...], preferred_element_type=F32)
    z = a * jax.nn.sigmoid(a) * b
    acc_scr[...] += jnp.dot(z.astype(BF16), wd_ref[...], preferred_element_type=F32)

    @pl.when(j == pl.num_programs(1) - 1)
    def _():
        o_ref[...] = x_ref[...] + 0.5 * acc_scr[...]


def _ffn(x, g, wg, wu, wd):
    m, d = x.shape
    f = wg.shape[1]
    tm = min(ROW_TILE, m)
    tf = f // 2
    assert m % tm == 0 and f % tf == 0 and tf % LANES == 0
    return pl.pallas_call(
        _ffn_body,
        grid=(m // tm, f // tf),
        in_specs=[
            pl.BlockSpec((tm, d), lambda i, j: (i, 0)),
            pl.BlockSpec((1, d), lambda i, j: (0, 0)),
            pl.BlockSpec((d, tf), lambda i, j: (0, j)),
            pl.BlockSpec((d, tf), lambda i, j: (0, j)),
            pl.BlockSpec((tf, d), lambda i, j: (j, 0)),
        ],
        out_specs=pl.BlockSpec((tm, d), lambda i, j: (i, 0)),
        out_shape=jax.ShapeDtypeStruct((m, d), F32),
        scratch_shapes=[pltpu.VMEM((tm, d), BF16), pltpu.VMEM((tm, d), F32)],
        compiler_params=_cparams(("parallel", "arbitrary")),
        name="ffn",
    )(x, g, wg, wu, wd)


def _proj_body(*refs, segs, n_norm, has_rot):
    it = iter(refs)
    x_ref, g_ref, w_ref = next(it), next(it), next(it)
    gains_ref = bd_ref = cos_ref = sin_ref = None
    if n_norm:
        gains_ref, bd_ref = next(it), next(it)
    if has_rot:
        cos_ref, sin_ref = next(it), next(it)
    outs = list(it)
    h = _rms_rows(x_ref[...], g_ref[...]).astype(BF16)
    off = oi = ni = 0
    for width, kind, scale_f32, scale_bf16, want_f32, want_bf16 in segs:
        y = jnp.dot(h, w_ref[:, off:off + width], preferred_element_type=F32)
        off += width
        if kind == "norm":
            y2 = y * y
            hi = y2.astype(BF16)
            lo = (y2 - hi.astype(F32)).astype(BF16)
            ss = (jnp.dot(hi, bd_ref[...], preferred_element_type=F32)
                  + jnp.dot(lo, bd_ref[...], preferred_element_type=F32))
            y = y * lax.rsqrt(ss * (1.0 / HEAD_DIM) + NORM_EPS) * gains_ref[ni:ni + 1, :]
            ni += 1
        elif kind == "rot":
            lane = lax.broadcasted_iota(jnp.int32, y.shape, 1)
            first_half = (lane % HEAD_DIM) < (HEAD_DIM // 2)
            swapped = jnp.where(first_half, pltpu.roll(y, width - HEAD_DIM // 2, 1),
                                pltpu.roll(y, HEAD_DIM // 2, 1))
            y = y * cos_ref[...] + swapped * sin_ref[...]
        if scale_f32 != 1.0:
            y = y * scale_f32
        if want_f32:
            outs[oi][...] = y
            oi += 1
        if want_bf16:
            outs[oi][...] = (y * scale_bf16).astype(BF16) if scale_bf16 != 1.0 else y.astype(BF16)
            oi += 1


def _proj(x, g, w, segs, gains=None, bd=None, cos=None, sin=None):
    m, d = x.shape
    tm = min(ROW_TILE, m)
    assert m % tm == 0
    wtot = w.shape[1]
    n_norm = sum(1 for s in segs if s[1] == "norm")
    has_rot = any(s[1] == "rot" for s in segs)
    args = [x, g, w]
    in_specs = [
        pl.BlockSpec((tm, d), lambda i: (i, 0)),
        pl.BlockSpec((1, d), lambda i: (0, 0)),
        pl.BlockSpec((d, wtot), lambda i: (0, 0)),
    ]
    if n_norm:
        args += [gains, bd]
        in_specs += [pl.BlockSpec(gains.shape, lambda i: (0, 0)), pl.BlockSpec(bd.shape, lambda i: (0, 0))]
    if has_rot:
        args += [cos, sin]
        in_specs += [pl.BlockSpec((tm, cos.shape[1]), lambda i: (i, 0)),
                     pl.BlockSpec((tm, sin.shape[1]), lambda i: (i, 0))]
    out_shape, out_specs = [], []
    for width, _, _, _, want_f32, want_bf16 in segs:
        for want, dt in ((want_f32, F32), (want_bf16, BF16)):
            if want:
                out_shape.append(jax.ShapeDtypeStruct((m, width), dt))
                out_specs.append(pl.BlockSpec((tm, width), lambda i: (i, 0)))
    return pl.pallas_call(
        functools.partial(_proj_body, segs=segs, n_norm=n_norm, has_rot=has_rot),
        grid=(m // tm,),
        in_specs=in_specs,
        out_specs=out_specs,
        out_shape=out_shape,
        compiler_params=_cparams(("parallel",)),
        name="proj",
    )(*args)


def _bias_tile_body(offs_ref, rb_ref, o_ref, *, rmul, head0, mult):
    k = pl.program_id(0)
    hd = head0 + pl.program_id(1)
    tq, tk = o_ref.shape
    r = lax.broadcasted_iota(jnp.int32, (tq, tk), 0) * rmul
    c = lax.broadcasted_iota(jnp.int32, (tq, tk), 1)
    rel = r - c + offs_ref[k]
    b_far = rb_ref[N_BUCKETS - 1, hd]
    val = jnp.full((tq, tk), (rb_ref[0, hd] - b_far) * mult, F32)
    for t in range(1, N_BUCKETS):
        val = jnp.where(rel >= T5_THR[t], (rb_ref[t, hd] - b_far) * mult, val)
    o_ref[...] = jnp.where(rel >= 0, val, NEG)


def _bias_tiles(rel_bias, offs, n_heads, head0, tq, tk, rmul, mult=1.0):
    offs = jnp.asarray(np.asarray(offs, np.int32))
    nk = offs.shape[0]
    return pl.pallas_call(
        functools.partial(_bias_tile_body, rmul=rmul, head0=head0, mult=mult),
        grid_spec=pltpu.PrefetchScalarGridSpec(
            num_scalar_prefetch=1,
            grid=(nk, n_heads),
            in_specs=[pl.BlockSpec(memory_space=pltpu.SMEM)],
            out_specs=pl.BlockSpec((None, None, tq, tk), lambda k, h, offs: (k, h, 0, 0)),
        ),
        out_shape=jax.ShapeDtypeStruct((nk, n_heads, tq, tk), F32),
        compiler_params=_cparams(("arbitrary", "arbitrary")),
        name="bias_tiles",
    )(offs, rel_bias)


def _tile_schedule(t, tq, tk):
    offs_kind = {}
    far_off = 1 << 24
    qi, kj, kind, flags = [], [], [], []
    for i in range(t // tq):
        js = [j for j in range(t // tk) if j * tk <= i * tq + tq - 1]
        for n, j in enumerate(js):
            off = i * tq - j * tk
            if off - (tk - 1) >= T5_FAR:
                off = far_off
            kd = offs_kind.setdefault(off, len(offs_kind))
            qi.append(i)
            kj.append(j)
            kind.append(kd)
            flags.append((1 if n == 0 else 0) | (2 if n == len(js) - 1 else 0))
    offs = [o for o, _ in sorted(offs_kind.items(), key=lambda kv: kv[1])]
    far_kind = offs_kind.get(far_off)
    for n in reversed(range(len(kind))):
        if kind[n] == far_kind:
            flags[n] |= 4
            kind[n] = kind[n + 1]
    as_i32 = lambda v: jnp.asarray(np.asarray(v, np.int32))
    return as_i32(qi), as_i32(kj), as_i32(kind), as_i32(flags), offs, far_kind is not None


def _lambda_value(lam_ref, lam_init):
    lv = lam_ref[...]
    s1 = jnp.sum(lv[0:1, :] * lv[1:2, :], axis=-1, keepdims=True)
    s2 = jnp.sum(lv[2:3, :] * lv[3:4, :], axis=-1, keepdims=True)
    return jnp.exp(s1) - jnp.exp(s2) + lam_init


def _sweep_by_kind(sweep, flag, has_far):
    if not has_far:
        sweep(True)
    else:
        pl.when((flag & 4) != 0)(functools.partial(sweep, False))
        pl.when((flag & 4) == 0)(functools.partial(sweep, True))


def _attn_a_body(qi_ref, kj_ref, kind_ref, fl_ref, q_ref, k_ref, v_ref, b_ref, lam_ref, sg_ref, o_ref,
                 m_scr, l_scr, acc_scr, *, lam_init, has_far):
    t = pl.program_id(0)
    flag = fl_ref[t]

    @pl.when((flag & 1) != 0)
    def _():
        m_scr[...] = jnp.full_like(m_scr, NEG)
        l_scr[...] = jnp.zeros_like(l_scr)
        acc_scr[...] = jnp.zeros_like(acc_scr)

    hw = 2 * HEAD_DIM
    reps = k_ref.shape[0] // LANES
    lane = lax.broadcasted_iota(jnp.int32, (q_ref.shape[0], hw), 1)

    def sweep(with_bias):
        for h in range(H_A):
            qh = q_ref[:, h * hw:(h + 1) * hw]
            kh = k_ref[:, h * hw:(h + 1) * hw]
            vh = v_ref[:, h * hw:(h + 1) * hw]
            for c in range(2):
                i = 2 * h + c
                comp = (lane < HEAD_DIM) if c == 0 else (lane >= HEAD_DIM)
                s = _dot_nt(jnp.where(comp, qh, jnp.zeros_like(qh)), kh)
                if with_bias:
                    s = s + b_ref[h]
                m_prev = m_scr[i]
                m_new = jnp.maximum(m_prev, jnp.max(s, axis=-1, keepdims=True))
                alpha = jnp.exp2(m_prev - m_new)
                p = jnp.exp2(s - jnp.concatenate([m_new] * reps, axis=1))
                l_scr[i] = alpha * l_scr[i] + jnp.sum(p, axis=-1, keepdims=True)
                acc_scr[i] = alpha * acc_scr[i] + jnp.dot(p.astype(BF16), vh, preferred_element_type=F32)
                m_scr[i] = m_new

    _sweep_by_kind(sweep, flag, has_far)

    @pl.when((flag & 2) != 0)
    def _():
        lam = _lambda_value(lam_ref, lam_init)
        for h in range(H_A):
            o = acc_scr[2 * h] / l_scr[2 * h] - lam * (acc_scr[2 * h + 1] / l_scr[2 * h + 1])
            o = _rms_rows(o, sg_ref[...]) * (1.0 - lam_init)
            o_ref[:, h * hw:(h + 1) * hw] = o.astype(BF16)


def _attn_a(q, k, v, bias, sched, lam_vecs, subln_g, lam_init):
    t = q.shape[0]
    tq = tk = min(ATT_A_TILE, t)
    qi, kj, kind, flags, _, has_far = sched
    n = qi.shape[0]
    return pl.pallas_call(
        functools.partial(_attn_a_body, lam_init=lam_init, has_far=has_far),
        grid_spec=pltpu.PrefetchScalarGridSpec(
            num_scalar_prefetch=4,
            grid=(n,),
            in_specs=[
                pl.BlockSpec((tq, W_A), lambda s, qi, kj, kd, fl: (qi[s], 0)),
                pl.BlockSpec((tk, W_A), lambda s, qi, kj, kd, fl: (kj[s], 0)),
                pl.BlockSpec((tk, W_A), lambda s, qi, kj, kd, fl: (kj[s], 0)),
                pl.BlockSpec((None, H_A, tq, tk), lambda s, qi, kj, kd, fl: (kd[s], 0, 0, 0)),
                pl.BlockSpec((4, HEAD_DIM), lambda s, qi, kj, kd, fl: (0, 0)),
                pl.BlockSpec((1, 2 * HEAD_DIM), lambda s, qi, kj, kd, fl: (0, 0)),
            ],
            out_specs=pl.BlockSpec((tq, W_A), lambda s, qi, kj, kd, fl: (qi[s], 0)),
            scratch_shapes=[pltpu.VMEM((2 * H_A, tq, LANES), F32), pltpu.VMEM((2 * H_A, tq, LANES), F32),
                            pltpu.VMEM((2 * H_A, tq, 2 * HEAD_DIM), F32)],
        ),
        out_shape=jax.ShapeDtypeStruct((t, W_A), BF16),
        compiler_params=_cparams(("arbitrary",)),
        name="attn_a",
    )(qi, kj, kind, flags, q, k, v, bias, lam_vecs, subln_g)


def _ret_constants(chunk):
    log_g = np.log(1.0 - 2.0 ** (-5.0 - np.arange(H_B, dtype=np.float64)))
    i = np.arange(chunk, dtype=np.float64)
    d = i[:, None] - i[None, :]
    decay = np.where(d >= 0, np.exp(np.maximum(d, 0.0)[None] * log_g[:, None, None]), 0.0)
    head_of_lane = np.arange(W_BQK) // DK_B
    q_w = np.exp((i[:, None] + 1.0) * log_g[head_of_lane][None, :])
    k_w_t = np.exp((chunk - 1 - i)[None, :] * log_g[head_of_lane][:, None])
    g_c = np.exp(chunk * log_g)[head_of_lane]
    g_c = np.broadcast_to(g_c.reshape(H_B // 2, 2 * DK_B, 1), (H_B // 2, 2 * DK_B, DV_B))
    f = lambda a: jnp.asarray(np.asarray(a, np.float32))
    return f(decay), f(q_w), f(k_w_t), f(g_c), np.exp(log_g)


def _ret_body(q_ref, k_ref, kt_ref, v_ref, gb_ref, s0_ref, dec_ref, qw_ref, kwt_ref, gc_ref,
              o_ref, sfin_ref, s_scr):
    n = pl.program_id(0)

    @pl.when(n == 0)
    def _():
        s_scr[...] = s0_ref[...]

    pw = 2 * DK_B
    lane = lax.broadcasted_iota(jnp.int32, (q_ref.shape[0], pw), 1)
    row = lax.broadcasted_iota(jnp.int32, (pw, kt_ref.shape[1]), 0)
    for g in range(H_B // 2):
        s_prev = s_scr[g]
        s_prev_bf = s_prev.astype(BF16)
        qp = q_ref[:, g * pw:(g + 1) * pw]
        kp = k_ref[:, g * pw:(g + 1) * pw].astype(BF16)
        qwp = qp * qw_ref[:, g * pw:(g + 1) * pw]
        ktw = kt_ref[g * pw:(g + 1) * pw, :] * kwt_ref[g * pw:(g + 1) * pw, :]
        upd = jnp.zeros_like(s_prev)
        for hh in range(2):
            h = 2 * g + hh
            in_head = (lane < DK_B) if hh == 0 else (lane >= DK_B)
            in_head_r = (row < DK_B) if hh == 0 else (row >= DK_B)
            vh = v_ref[:, h * DV_B:(h + 1) * DV_B].astype(BF16)
            sc = _dot_nt(jnp.where(in_head, qp, 0.0).astype(BF16), kp) * dec_ref[h]
            o = (jnp.dot(sc.astype(BF16), vh, preferred_element_type=F32)
                 + jnp.dot(jnp.where(in_head, qwp, 0.0).astype(BF16), s_prev_bf, preferred_element_type=F32))
            o = o * lax.rsqrt(jnp.mean(o * o, axis=-1, keepdims=True) + NORM_EPS)
            gate = gb_ref[:, h * DV_B:(h + 1) * DV_B]
            o_ref[:, h * DV_B:(h + 1) * DV_B] = (o * (gate * jax.nn.sigmoid(gate))).astype(BF16)
            upd = upd + jnp.dot(jnp.where(in_head_r, ktw, 0.0).astype(BF16), vh, preferred_element_type=F32)
        s_scr[g] = gc_ref[g] * s_prev + upd

    @pl.when(n == pl.num_programs(0) - 1)
    def _():
        sfin_ref[...] = s_scr[...]


def _retention(q, k, v, gb, s0_pairs, consts):
    t = q.shape[0]
    c = RET_CHUNK
    decay, q_w, k_w_t, g_c, _ = consts
    kt = k.T
    full = lambda a: pl.BlockSpec(a.shape, lambda n: (0,) * a.ndim)
    return pl.pallas_call(
        _ret_body,
        grid=(t // c,),
        in_specs=[
            pl.BlockSpec((c, W_BQK), lambda n: (n, 0)),
            pl.BlockSpec((c, W_BQK), lambda n: (n, 0)),
            pl.BlockSpec((W_BQK, c), lambda n: (0, n)),
            pl.BlockSpec((c, W_BV), lambda n: (n, 0)),
            pl.BlockSpec((c, W_BV), lambda n: (n, 0)),
            full(s0_pairs), full(decay), full(q_w), full(k_w_t), full(g_c),
        ],
        out_specs=[pl.BlockSpec((c, W_BV), lambda n: (n, 0)), full(s0_pairs)],
        out_shape=[jax.ShapeDtypeStruct((t, W_BV), BF16), jax.ShapeDtypeStruct(s0_pairs.shape, F32)],
        scratch_shapes=[pltpu.VMEM(s0_pairs.shape, F32)],
        compiler_params=_cparams(("arbitrary",)),
        name="retention",
    )(q, k, kt, v, gb, s0_pairs, decay, q_w, k_w_t, g_c)


def _kmean_body(k_ref, o_ref):
    nb = o_ref.shape[0]
    kk = k_ref[...].reshape(nb, MOBA_BLOCK, k_ref.shape[1])
    o_ref[...] = jnp.sum(kk, axis=1) * (1.0 / MOBA_BLOCK)


def _kmean(k):
    t, w = k.shape
    nblk = t // MOBA_BLOCK
    nb = 8
    assert nblk % nb == 0
    return pl.pallas_call(
        _kmean_body,
        grid=(nblk // nb,),
        in_specs=[pl.BlockSpec((nb * MOBA_BLOCK, w), lambda i: (i, 0))],
        out_specs=pl.BlockSpec((nb, w), lambda i: (i, 0)),
        out_shape=jax.ShapeDtypeStruct((nblk, w), F32),
        compiler_params=_cparams(("parallel",)),
        name="moba_kmean",
    )(k)


def _top_blocks(gate, valid):
    lane = lax.broadcasted_iota(jnp.int32, gate.shape, 1).astype(F32)
    g = jnp.where(valid, gate, -jnp.inf)
    picked = jnp.zeros(gate.shape, F32)
    idxs = []
    for _ in range(MOBA_TOPK):
        m = jnp.max(g, axis=-1, keepdims=True)
        cand = jnp.logical_and(g == m, m > -jnp.inf)
        idx = jnp.min(jnp.where(cand, lane, float(LANES)), axis=-1, keepdims=True)
        pick = lane == idx
        picked = jnp.where(pick, 1.0, picked)
        g = jnp.where(pick, -jnp.inf, g)
        idxs.append(idx)
    return picked, idxs


def _select_body(q_ref, km_ref, o_ref, *, pos0):
    tq = q_ref.shape[0]
    q = q_ref[...]
    km = km_ref[...]
    pw = 2 * HEAD_DIM
    blk = lax.broadcasted_iota(jnp.int32, (tq, LANES), 1)
    qpos = pos0 + pl.program_id(0) * tq + lax.broadcasted_iota(jnp.int32, (tq, LANES), 0)
    past = blk < qpos // MOBA_BLOCK
    for h in range(H_C):
        g, hh = h // 2, h % 2
        in_head = (blk < HEAD_DIM) if hh == 0 else (blk >= HEAD_DIM)
        qp = q[:, g * pw:(g + 1) * pw]
        gate = _dot_nt(jnp.where(in_head, qp, 0.0), km[:, g * pw:(g + 1) * pw], precision=lax.Precision.HIGHEST)
        picked, _ = _top_blocks(gate, past)
        sel = jnp.where(jnp.logical_and(past, picked == 0.0), NEG, 0.0)
        if hh == 0:
            sel = pltpu.roll(sel, HEAD_DIM, 1)
        qa = jnp.where(in_head, qp * (QK_SCALE * LOG2E), sel)
        o_ref[:, h * LANES:(h + 1) * LANES] = qa.astype(BF16)


def _select(q, kmean_pad, pos0):
    t, w = q.shape
    tq = min(ROW_TILE, t)
    assert (pos0 + t) // MOBA_BLOCK <= HEAD_DIM
    return pl.pallas_call(
        functools.partial(_select_body, pos0=pos0),
        grid=(t // tq,),
        in_specs=[pl.BlockSpec((tq, w), lambda i: (i, 0)),
                  pl.BlockSpec(kmean_pad.shape, lambda i: (0, 0))],
        out_specs=pl.BlockSpec((tq, H_C * LANES), lambda i: (i, 0)),
        out_shape=jax.ShapeDtypeStruct((t, H_C * LANES), BF16),
        compiler_params=_cparams(("parallel",)),
        name="moba_select",
    )(q, kmean_pad)


def _attn_c_body(qi_ref, kj_ref, kind_ref, fl_ref, q_ref, k_ref, v_ref, b_ref, o_ref, m_scr, acc_scr, *, has_far):
    t = pl.program_id(0)
    flag = fl_ref[t]
    j = kj_ref[t]

    @pl.when((flag & 1) != 0)
    def _():
        m_scr[...] = jnp.full_like(m_scr, NEG)
        acc_scr[...] = jnp.zeros_like(acc_scr)

    pw = 2 * HEAD_DIM
    tk = k_ref.shape[0]
    reps = tk // LANES
    lane_k = lax.broadcasted_iota(jnp.int32, (tk, pw), 1)

    def sweep(with_bias):
        for h in range(H_C):
            g, hh = h // 2, h % 2
            kp = k_ref[:, g * pw:(g + 1) * pw]
            vp = v_ref[:, g * pw:(g + 1) * pw]
            in_head = (lane_k < HEAD_DIM) if hh == 0 else (lane_k >= HEAD_DIM)
            onehot = lane_k == j + (HEAD_DIM if hh == 0 else 0)
            ka = jnp.where(in_head, kp, jnp.where(onehot, 1.0, 0.0).astype(BF16))
            va = jnp.where(in_head, vp, jnp.ones_like(vp))
            s = _dot_nt(q_ref[:, h * LANES:(h + 1) * LANES], ka)
            if with_bias:
                s = s + b_ref[h]
            m_prev = m_scr[h]
            m_new = jnp.maximum(m_prev, jnp.max(s, axis=-1, keepdims=True))
            alpha = jnp.exp2(m_prev - m_new)
            p = jnp.exp2(s - jnp.concatenate([m_new] * reps, axis=1))
            acc_scr[h] = alpha * acc_scr[h] + jnp.dot(p.astype(BF16), va, preferred_element_type=F32)
            m_scr[h] = m_new

    _sweep_by_kind(sweep, flag, has_far)

    @pl.when((flag & 2) != 0)
    def _():
        lane = lax.broadcasted_iota(jnp.int32, (q_ref.shape[0], pw), 1)
        for g in range(H_C // 2):
            a0, a1 = acc_scr[2 * g], acc_scr[2 * g + 1]
            o0 = a0 / pltpu.roll(a0, HEAD_DIM, 1)
            o1 = a1 / pltpu.roll(a1, HEAD_DIM, 1)
            o_ref[:, g * pw:(g + 1) * pw] = jnp.where(lane < HEAD_DIM, o0, o1).astype(BF16)


def _attn_c(q_aug, k, v, bias, sched):
    t = q_aug.shape[0]
    tq = min(ATT_C_TQ, t)
    tk = MOBA_BLOCK
    qi, kj, kind, flags, _, has_far = sched
    n = qi.shape[0]
    return pl.pallas_call(
        functools.partial(_attn_c_body, has_far=has_far),
        grid_spec=pltpu.PrefetchScalarGridSpec(
            num_scalar_prefetch=4,
            grid=(n,),
            in_specs=[
                pl.BlockSpec((tq, H_C * LANES), lambda s, qi, kj, kd, fl: (qi[s], 0)),
                pl.BlockSpec((tk, W_C), lambda s, qi, kj, kd, fl: (kj[s], 0)),
                pl.BlockSpec((tk, W_C), lambda s, qi, kj, kd, fl: (kj[s], 0)),
                pl.BlockSpec((None, H_C, tq, tk), lambda s, qi, kj, kd, fl: (kd[s], 0, 0, 0)),
            ],
            out_specs=pl.BlockSpec((tq, W_C), lambda s, qi, kj, kd, fl: (qi[s], 0)),
            scratch_shapes=[pltpu.VMEM((H_C, tq, LANES), F32), pltpu.VMEM((H_C, tq, 2 * HEAD_DIM), F32)],
        ),
        out_shape=jax.ShapeDtypeStruct((t, W_C), BF16),
        compiler_params=_cparams(("arbitrary",)),
        name="attn_c",
    )(qi, kj, kind, flags, q_aug, k, v, bias)


def _merge_body(x_ref, g_ref, wg_ref, oa_ref, ob_ref, oc_ref, wa_ref, wb_ref, wc_ref, wo_ref, o_ref):
    x = x_ref[...]
    h = _rms_rows(x, g_ref[...]).astype(BF16)
    mix = None
    for n, (o_in, w_up) in enumerate(((oa_ref, wa_ref), (ob_ref, wb_ref), (oc_ref, wc_ref))):
        y = jnp.dot(o_in[...], w_up[...], preferred_element_type=F32)
        gl = jnp.dot(h, wg_ref[:, n * D_MODEL:(n + 1) * D_MODEL], preferred_element_type=F32)
        term = jax.nn.sigmoid(gl) * y
        mix = term if mix is None else mix + term
    o_ref[...] = x + jnp.dot(mix.astype(BF16), wo_ref[...], preferred_element_type=F32)


def _merge(x, g, wg, oa, ob, oc, wa, wb, wc, wo):
    m, d = x.shape
    tm = min(ROW_TILE, m)
    rows = lambda w: pl.BlockSpec((tm, w), lambda i: (i, 0))
    full = lambda a: pl.BlockSpec(a.shape, lambda i: (0, 0))
    return pl.pallas_call(
        _merge_body,
        grid=(m // tm,),
        in_specs=[rows(d), full(g), full(wg), rows(W_A), rows(W_BV), rows(W_C),
                  full(wa), full(wb), full(wc), full(wo)],
        out_specs=rows(d),
        out_shape=jax.ShapeDtypeStruct((m, d), F32),
        compiler_params=_cparams(("parallel",)),
        name="merge",
    )(x, g, wg, oa, ob, oc, wa, wb, wc, wo)


DEC_ROWS = 16


def _dec_a_body(pt_ref, q_ref, *refs, npg, lam_init):
    k_refs, v_refs = refs[:npg], refs[npg:2 * npg]
    b_ref, knew_ref, vnew_ref, b0_ref, lam_ref, sg_ref, o_ref, m_scr, l_scr, acc_scr = refs[2 * npg:]
    p = pl.program_id(1)
    prow = k_refs[0].shape[0]

    @pl.when(p == 0)
    def _():
        m_scr[...] = jnp.full_like(m_scr, NEG)
        l_scr[...] = jnp.zeros_like(l_scr)
        acc_scr[...] = jnp.zeros_like(acc_scr)

    qrows = q_ref[...].astype(BF16)
    s = jnp.concatenate([_dot_nt(qrows, k_refs[i][...].astype(BF16)) for i in range(npg)], axis=1) + b_ref[...]
    m_prev = m_scr[...]
    m_new = jnp.maximum(m_prev, jnp.max(s, axis=-1, keepdims=True))
    alpha = jnp.exp(m_prev - m_new)
    pr = jnp.exp(s - m_new).astype(BF16)
    l_scr[...] = alpha * l_scr[...] + jnp.sum(pr.astype(F32), axis=-1, keepdims=True)
    acc = alpha * acc_scr[...]
    for i in range(npg):
        acc = acc + jnp.dot(pr[:, i * prow:(i + 1) * prow], v_refs[i][...].astype(BF16),
                            preferred_element_type=F32)
    acc_scr[...] = acc
    m_scr[...] = m_new

    @pl.when(p == pl.num_programs(1) - 1)
    def _():
        k_new = knew_ref[...].astype(BF16).astype(F32)
        v_new = vnew_ref[...].astype(BF16).astype(F32)
        s_new = jnp.sum(qrows.astype(F32) * k_new, axis=-1, keepdims=True) + b0_ref[...]
        m_fin = jnp.maximum(m_scr[...], s_new)
        a_fin = jnp.exp(m_scr[...] - m_fin)
        p_new = jnp.exp(s_new - m_fin).astype(BF16).astype(F32)
        l_fin = a_fin * l_scr[...] + p_new
        o = (a_fin * acc_scr[...] + p_new * v_new) / l_fin
        lam = _lambda_value(lam_ref, lam_init)
        hw = 2 * HEAD_DIM
        for h in range(H_A):
            oh = o[2 * h:2 * h + 1, :] - lam * o[2 * h + 1:2 * h + 2, :]
            o_ref[:, h * hw:(h + 1) * hw] = _rms_rows(oh, sg_ref[...]) * (1.0 - lam_init)


def _dec_a(layer, page_table, qrows, cache_k, cache_v, bias, k_new, v_new, b0, lam_vecs, subln_g, lam_init):
    bsz, n_pages = page_table.shape
    prow = cache_k.shape[2]
    npg = min(PAGES_PER_STEP, n_pages)
    assert n_pages % npg == 0
    n_steps = n_pages // npg
    hw = 2 * HEAD_DIM
    seq_rows = pl.BlockSpec((None, DEC_ROWS, hw), lambda b, p, pt: (b, 0, 0))
    const = lambda a: pl.BlockSpec(a.shape, lambda b, p, pt: (0,) * a.ndim)

    def page_spec(i):
        return pl.BlockSpec((None, None, prow, hw), lambda b, p, pt: (layer, pt[b, p * npg + i], 0, 0))

    in_specs = ([seq_rows] + [page_spec(i) for i in range(npg)] + [page_spec(i) for i in range(npg)]
                + [pl.BlockSpec((None, DEC_ROWS, npg * prow), lambda b, p, pt: (p // (n_steps - 1) if n_steps > 1 else 1, 0, 0)),
                   seq_rows, seq_rows, const(b0), const(lam_vecs), const(subln_g)])
    return pl.pallas_call(
        functools.partial(_dec_a_body, npg=npg, lam_init=lam_init),
        grid_spec=pltpu.PrefetchScalarGridSpec(
            num_scalar_prefetch=1,
            grid=(bsz, n_steps),
            in_specs=in_specs,
            out_specs=pl.BlockSpec((None, 1, W_A), lambda b, p, pt: (b, 0, 0)),
            scratch_shapes=[pltpu.VMEM((DEC_ROWS, 1), F32), pltpu.VMEM((DEC_ROWS, 1), F32),
                            pltpu.VMEM((DEC_ROWS, hw), F32)],
        ),
        out_shape=jax.ShapeDtypeStruct((bsz, 1, W_A), F32),
        compiler_params=_cparams(("parallel", "arbitrary")),
        name="decode_attn_a",
    )(page_table, qrows, *([cache_k] * npg), *([cache_v] * npg), bias, k_new, v_new, b0, lam_vecs, subln_g)


def _dec_ret_body(qc_ref, kc_ref, v_ref, gb_ref, s0_ref, o_ref, s_ref, *, gammas):
    for h in range(H_B):
        vh = v_ref[:, h * DV_B:(h + 1) * DV_B]
        s_new = float(gammas[h]) * s0_ref[h] + kc_ref[h] * vh
        s_ref[h] = s_new
        o = jnp.sum(qc_ref[h] * s_new, axis=0, keepdims=True)
        o = o * lax.rsqrt(jnp.mean(o * o, axis=-1, keepdims=True) + NORM_EPS)
        gate = gb_ref[:, h * DV_B:(h + 1) * DV_B]
        o_ref[:, h * DV_B:(h + 1) * DV_B] = o * (gate * jax.nn.sigmoid(gate))


def _dec_ret(layer, q, k, v, gb, state, gammas):
    bsz = q.shape[0]
    qc = q.reshape(bsz, H_B, DK_B, 1)
    kc = k.reshape(bsz, H_B, DK_B, 1)
    col = pl.BlockSpec((None, H_B, DK_B, 1), lambda b: (b, 0, 0, 0))
    row = pl.BlockSpec((None, 1, W_BV), lambda b: (b, 0, 0))
    return pl.pallas_call(
        functools.partial(_dec_ret_body, gammas=gammas),
        grid=(bsz,),
        in_specs=[col, col, row, row,
                  pl.BlockSpec((None, None, H_B, DK_B, DV_B), lambda b: (layer, b, 0, 0, 0))],
        out_specs=[row, pl.BlockSpec((None, H_B, DK_B, DV_B), lambda b: (b, 0, 0, 0))],
        out_shape=[jax.ShapeDtypeStruct((bsz, 1, W_BV), F32),
                   jax.ShapeDtypeStruct((bsz, H_B, DK_B, DV_B), F32)],
        compiler_params=_cparams(("parallel",)),
        name="decode_retention",
    )(qc, kc, v.reshape(bsz, 1, W_BV), gb.reshape(bsz, 1, W_BV), state)


def _dec_c1_body(pt_ref, q_ref, *refs, npg, ppb, n_past):
    k_refs = refs[:npg]
    o_ref, km_scr = refs[npg:]
    p = pl.program_id(1)

    @pl.when(p == 0)
    def _():
        km_scr[...] = jnp.zeros_like(km_scr)

    blk_lane = lax.broadcasted_iota(jnp.int32, km_scr.shape, 1)
    km = km_scr[...]
    for i in range(npg // ppb):
        pages = k_refs[i * ppb][...]
        for u in range(1, ppb):
            pages = pages + k_refs[i * ppb + u][...]
        tot = jnp.sum(pages, axis=-1, keepdims=True)
        km = jnp.where(blk_lane == p * (npg // ppb) + i, tot * (1.0 / MOBA_BLOCK), km)
    km_scr[...] = km

    @pl.when(p == pl.num_programs(1) - 1)
    def _():
        prod = km_scr[...] * q_ref[...]
        gate = jnp.sum(prod.reshape(H_C, HEAD_DIM, LANES), axis=1)
        blk = lax.broadcasted_iota(jnp.int32, gate.shape, 1)
        _, idxs = _top_blocks(gate, blk < n_past)
        out = jnp.full(gate.shape, -1, jnp.int32)
        for r, idx in enumerate(idxs):
            out = jnp.where(jnp.logical_and(blk == r, idx < float(LANES)), idx.astype(jnp.int32), out)
        o_ref[...] = out


def _dec_c1(layer, page_table, qcol, cache_kt):
    bsz, n_pages = page_table.shape
    page = cache_kt.shape[3]
    ppb = MOBA_BLOCK // page
    npg = min(PAGES_PER_STEP, n_pages)
    assert MOBA_BLOCK % page == 0 and npg % ppb == 0 and n_pages % npg == 0
    n_past = n_pages // ppb
    assert n_past <= LANES

    def page_spec(i):
        return pl.BlockSpec((None, None, W_C, page), lambda b, p, pt: (layer, pt[b, p * npg + i], 0, 0))

    return pl.pallas_call(
        functools.partial(_dec_c1_body, npg=npg, ppb=ppb, n_past=n_past),
        grid_spec=pltpu.PrefetchScalarGridSpec(
            num_scalar_prefetch=1,
            grid=(bsz, n_pages // npg),
            in_specs=[pl.BlockSpec((None, W_C, 1), lambda b, p, pt: (b, 0, 0))] + [page_spec(i) for i in range(npg)],
            out_specs=pl.BlockSpec((None, H_C, LANES), lambda b, p, pt: (b, 0, 0)),
            scratch_shapes=[pltpu.VMEM((W_C, LANES), F32)],
        ),
        out_shape=jax.ShapeDtypeStruct((bsz, H_C, LANES), jnp.int32),
        compiler_params=_cparams(("parallel", "arbitrary")),
        name="decode_moba_select",
    )(page_table, qcol, *([cache_kt] * npg))


def _dec_c2_body(pt_ref, idx_ref, q_ref, *refs, ppb):
    nk = MOBA_TOPK * ppb
    k_refs, v_refs, b_refs = refs[:nk], refs[nk:2 * nk], refs[2 * nk:2 * nk + MOBA_TOPK]
    knew_ref, vnew_ref, b0_ref, o_ref = refs[2 * nk + MOBA_TOPK:]
    b, h = pl.program_id(0), pl.program_id(1)
    q = q_ref[...]
    q_rows = jnp.broadcast_to(q, (DEC_ROWS, HEAD_DIM)).astype(BF16)
    ss = []
    for r in range(MOBA_TOPK):
        valid = idx_ref[(b * H_C + h) * MOBA_TOPK + r] >= 0
        s = jnp.concatenate([jnp.dot(q_rows, k_refs[r * ppb + u][...].astype(BF16), preferred_element_type=F32)
                             for u in range(ppb)], axis=1) + b_refs[r][...]
        ss.append(jnp.where(valid, s, NEG))
    k_new = knew_ref[...].astype(BF16).astype(F32)
    v_new = vnew_ref[...].astype(BF16).astype(F32)
    s_new = jnp.sum(q.astype(BF16).astype(F32) * k_new, axis=-1, keepdims=True) + b0_ref[...][:, 0:1]
    m = s_new
    for s in ss:
        m = jnp.maximum(m, jnp.max(s, axis=-1, keepdims=True))
    p_new = jnp.exp(s_new - m).astype(BF16).astype(F32)
    l = p_new
    o = p_new * v_new
    page = k_refs[0].shape[1]
    for r, s in enumerate(ss):
        p = jnp.exp(s - m).astype(BF16)
        l = l + jnp.sum(p.astype(F32), axis=-1, keepdims=True)
        for u in range(ppb):
            o = o + _dot_nt(p[:, u * page:(u + 1) * page], v_refs[r * ppb + u][...].astype(BF16))
    o_ref[...] = (o / l)[0:1, :]


def _dec_c2(layer, page_table, idx, q, cache_kt, cache_vt, bias_blocks, k_new, v_new, b0):
    bsz, n_pages = page_table.shape
    page = cache_kt.shape[3]
    ppb = MOBA_BLOCK // page
    idx_flat = idx[:, :, :MOBA_TOPK].reshape(-1)

    def blk_of(b, h, r, ix):
        return jnp.maximum(ix[(b * H_C + h) * MOBA_TOPK + r], 0)

    def page_spec(r, u):
        return pl.BlockSpec((None, None, HEAD_DIM, page),
                            lambda b, h, pt, ix: (layer, pt[b, blk_of(b, h, r, ix) * ppb + u], h, 0))

    def bias_spec(r):
        return pl.BlockSpec((None, None, 1, MOBA_BLOCK), lambda b, h, pt, ix: (blk_of(b, h, r, ix), h, 0, 0))

    seq_head = pl.BlockSpec((None, None, 1, HEAD_DIM), lambda b, h, pt, ix: (b, h, 0, 0))
    pages = [page_spec(r, u) for r in range(MOBA_TOPK) for u in range(ppb)]
    return pl.pallas_call(
        functools.partial(_dec_c2_body, ppb=ppb),
        grid_spec=pltpu.PrefetchScalarGridSpec(
            num_scalar_prefetch=2,
            grid=(bsz, H_C),
            in_specs=([seq_head] + pages + pages + [bias_spec(r) for r in range(MOBA_TOPK)]
                      + [seq_head, seq_head, pl.BlockSpec((None, 1, HEAD_DIM), lambda b, h, pt, ix: (h, 0, 0))]),
            out_specs=seq_head,
        ),
        out_shape=jax.ShapeDtypeStruct((bsz, H_C, 1, HEAD_DIM), F32),
        compiler_params=_cparams(("parallel", "arbitrary")),
        name="decode_moba_attn",
    )(page_table, idx_flat, q, *([cache_kt] * (MOBA_TOPK * ppb)), *([cache_vt] * (MOBA_TOPK * ppb)),
      *([bias_blocks] * MOBA_TOPK), k_new, v_new, b0)


def _rotary_tables(pos):
    half = HEAD_DIM // 2
    inv = 1.0 / (10000.0 ** (jnp.arange(half, dtype=F32) / half))
    ang = pos.astype(F32)[:, None] * inv[None, :]
    cos, sin = jnp.cos(ang), jnp.sin(ang)
    cos_t = jnp.tile(jnp.concatenate([cos, cos], axis=1), (1, H_B))
    sin_t = jnp.tile(jnp.concatenate([-sin, sin], axis=1), (1, H_B))
    return cos_t, sin_t


def _block_diag_ones():
    i = np.arange(W_A) // HEAD_DIM
    return jnp.asarray((i[:, None] == i[None, :]).astype(np.float32), dtype=BF16)


SEG_A = ((W_A, "norm", 1.0, QK_SCALE, False, True),
         (W_A, "norm", 1.0, 1.0, True, True),
         (W_A, "plain", 1.0, 1.0, True, True))
SEG_A_PROMPT = ((W_A, "norm", 1.0, QK_SCALE * LOG2E, False, True),) + SEG_A[1:]
SEG_B = ((W_BQK, "rot", 1.0, 1.0, True, False),
         (W_BQK, "rot", DK_B ** -0.5, 1.0, True, False),
         (W_BV, "plain", 1.0, 1.0, True, False),
         (W_BV, "plain", 1.0, 1.0, True, False))
SEG_C = ((W_C, "norm", 1.0, QK_SCALE, True, True),
         (W_C, "norm", 1.0, 1.0, True, True),
         (W_C, "plain", 1.0, 1.0, True, True))
SEG_C_PROMPT = ((W_C, "norm", 1.0, 1.0, True, False),) + SEG_C[1:]


def _layer_weights(l, w):
    bf = lambda a: a.astype(BF16)
    o_b, o_c, o_g = 3 * W_A, 3 * W_A + 2 * W_BQK + 2 * W_BV, 3 * W_A + 2 * W_BQK + 2 * W_BV + 3 * W_C
    w_in = w["w_in"][l]
    tile = lambda g: jnp.tile(g, W_A // HEAD_DIM)[None, :]
    return dict(
        ffn1=(w["ln_ffn1_g"][l][None, :], bf(w["ffn1_w_gate"][l]), bf(w["ffn1_w_up"][l]), bf(w["ffn1_w_down"][l])),
        ffn2=(w["ln_ffn2_g"][l][None, :], bf(w["ffn2_w_gate"][l]), bf(w["ffn2_w_up"][l]), bf(w["ffn2_w_down"][l])),
        ln_mix=w["ln_mix_g"][l][None, :],
        w_a=bf(w_in[:, :o_b]), w_b=bf(w_in[:, o_b:o_c]), w_c=bf(w_in[:, o_c:o_g]), w_g=bf(w_in[:, o_g:]),
        gains_a=jnp.concatenate([tile(w["qn_a_g"][l]), tile(w["kn_a_g"][l])], axis=0),
        gains_c=jnp.concatenate([tile(w["qn_c_g"][l]), tile(w["kn_c_g"][l])], axis=0),
        lam_vecs=jnp.stack([w["lam_q1"][l], w["lam_k1"][l], w["lam_q2"][l], w["lam_k2"][l]]),
        subln=w["subln_a_g"][l][None, :],
        w_up_a=bf(w["w_up_a"][l]), w_up_b=bf(w["w_up_b"][l]), w_up_c=bf(w["w_up_c"][l]), w_o=bf(w["w_o"][l]),
    )


def kernel(x_prompt, x_sample, cache_a_k, cache_a_v, cache_c_k, cache_c_v, state_ret, page_table, ln_ffn1_g, ffn1_w_gate, ffn1_w_up, ffn1_w_down, ln_mix_g, w_in, qn_a_g, kn_a_g, lam_q1, lam_k1, lam_q2, lam_k2, subln_a_g, qn_c_g, kn_c_g, rel_bias, w_up_a, w_up_b, w_up_c, w_o, ln_ffn2_g, ffn2_w_gate, ffn2_w_up, ffn2_w_down):
    w = dict(ln_ffn1_g=ln_ffn1_g, ffn1_w_gate=ffn1_w_gate, ffn1_w_up=ffn1_w_up, ffn1_w_down=ffn1_w_down,
             ln_mix_g=ln_mix_g, w_in=w_in, qn_a_g=qn_a_g, kn_a_g=kn_a_g, lam_q1=lam_q1, lam_k1=lam_k1,
             lam_q2=lam_q2, lam_k2=lam_k2, subln_a_g=subln_a_g, qn_c_g=qn_c_g, kn_c_g=kn_c_g,
             w_up_a=w_up_a, w_up_b=w_up_b, w_up_c=w_up_c, w_o=w_o, ln_ffn2_g=ln_ffn2_g,
             ffn2_w_gate=ffn2_w_gate, ffn2_w_up=ffn2_w_up, ffn2_w_down=ffn2_w_down)
    depth = w_in.shape[0]
    assert x_prompt.shape[0] == 1 and x_sample.shape[1] == 1
    t = x_prompt.shape[1]
    bsz, n_pages = page_table.shape
    page = cache_a_k.shape[2]
    n_pool = cache_a_k.shape[1]
    past = n_pages * page
    assert t % MOBA_BLOCK == 0 and past % MOBA_BLOCK == 0 and t % RET_CHUNK == 0

    bd = _block_diag_ones()
    cos_p, sin_p = _rotary_tables(jnp.arange(t, dtype=jnp.int32))
    cos_s, sin_s = _rotary_tables(jnp.full((bsz,), past, jnp.int32))
    ret_consts = _ret_constants(RET_CHUNK)
    gammas = ret_consts[4]

    ta = min(ATT_A_TILE, t)
    sched_a = _tile_schedule(t, ta, ta)
    bias_a = _bias_tiles(rel_bias, sched_a[4], H_A, 0, ta, ta, 1, LOG2E)
    tcq = min(ATT_C_TQ, t)
    sched_c = _tile_schedule(t, tcq, MOBA_BLOCK)
    bias_c = _bias_tiles(rel_bias, sched_c[4], H_C, H_A, tcq, MOBA_BLOCK, 1, LOG2E)

    npg = min(PAGES_PER_STEP, n_pages)
    dec_steps = n_pages // npg
    assert npg * page >= T5_FAR
    bias_da = _bias_tiles(rel_bias, [1 << 24, past - (dec_steps - 1) * npg * page], H_A, 0, 8, npg * page, 0)
    bias_da = jnp.repeat(bias_da[:, :, 0, :], 2, axis=1)
    same_head = (jnp.arange(2 * H_A) // 2)[:, None, None] == jnp.arange(H_A)[None, None, :]
    bias_da = jnp.where(same_head[None], bias_da[..., None], NEG).reshape(2, 2 * H_A, npg * page * H_A)
    bias_da = jnp.concatenate([bias_da, jnp.zeros_like(bias_da)], axis=1)
    n_past_blk = past // MOBA_BLOCK
    bias_dc = _bias_tiles(rel_bias, [past - j * MOBA_BLOCK for j in range(n_past_blk)], H_C, H_A, 8, MOBA_BLOCK, 0)
    bias_dc = bias_dc[:, :, 0:1, :]
    b_self = rel_bias[0] - rel_bias[N_BUCKETS - 1]
    b0_a = jnp.concatenate([jnp.repeat(b_self[:H_A], 2), jnp.zeros((8,), F32)])[:, None]
    b0_c = jnp.broadcast_to(b_self[H_A:, None, None], (H_C, 1, HEAD_DIM))

    ck_a = cache_a_k.reshape(depth, n_pool, page * H_A, 2 * HEAD_DIM)
    cv_a = cache_a_v.reshape(depth, n_pool, page * H_A, 2 * HEAD_DIM)
    ck_c = jnp.transpose(cache_c_k, (0, 1, 3, 4, 2)).reshape(depth, n_pool, W_C, page)
    cv_c = jnp.transpose(cache_c_v, (0, 1, 3, 4, 2)).reshape(depth, n_pool, W_C, page)
    comp_mask = jnp.eye(2, dtype=F32)[None, None, :, :, None]

    xp = x_prompt[0]
    xs = x_sample[:, 0]
    rows_p, rows_s = [], []
    for l in range(depth):
        lw = _layer_weights(l, w)
        lam_init = 0.8 - 0.6 * math.exp(-0.3 * l)

        xp = _ffn(xp, *lw["ffn1"])
        qa_bf, ka, ka_bf, va, va_bf = _proj(xp, lw["ln_mix"], lw["w_a"], SEG_A_PROMPT, gains=lw["gains_a"], bd=bd)
        qb, kb, vb, gb = _proj(xp, lw["ln_mix"], lw["w_b"], SEG_B, cos=cos_p, sin=sin_p)
        qc, kc, kc_bf, vc, vc_bf = _proj(xp, lw["ln_mix"], lw["w_c"], SEG_C_PROMPT, gains=lw["gains_c"], bd=bd)
        oa = _attn_a(qa_bf, ka_bf, va_bf, bias_a, sched_a, lw["lam_vecs"], lw["subln"], lam_init)
        s0 = jnp.zeros((H_B // 2, 2 * DK_B, DV_B), F32)
        ob, s_fin = _retention(qb, kb, vb, gb, s0, ret_consts)
        kmean = _kmean(kc)
        kmean_pad = jnp.pad(kmean, ((0, LANES - kmean.shape[0]), (0, 0)))
        qc_aug = _select(qc, kmean_pad, 0)
        oc = _attn_c(qc_aug, kc_bf, vc_bf, bias_c, sched_c)
        xp = _merge(xp, lw["ln_mix"], lw["w_g"], oa, ob, oc, lw["w_up_a"], lw["w_up_b"], lw["w_up_c"], lw["w_o"])
        xp = _ffn(xp, *lw["ffn2"])
        rows_p.append((ka.reshape(1, t, H_A, 2 * HEAD_DIM), va.reshape(1, t, H_A, 2 * HEAD_DIM),
                       kc.reshape(1, t, H_C, HEAD_DIM), vc.reshape(1, t, H_C, HEAD_DIM),
                       s_fin.reshape(1, H_B, DK_B, DV_B)))

        xs = _ffn(xs, *lw["ffn1"])
        sqa_bf, ska, _, sva, _ = _proj(xs, lw["ln_mix"], lw["w_a"], SEG_A, gains=lw["gains_a"], bd=bd)
        sqb, skb, svb, sgb = _proj(xs, lw["ln_mix"], lw["w_b"], SEG_B, cos=cos_s, sin=sin_s)
        sqc, sqc_bf, skc, _, svc, _ = _proj(xs, lw["ln_mix"], lw["w_c"], SEG_C, gains=lw["gains_c"], bd=bd)
        pad_rows = lambda a: jnp.pad(a, ((0, 0), (0, DEC_ROWS - 2 * H_A), (0, 0)))
        q_rows = (sqa_bf.astype(F32).reshape(bsz, H_A, 1, 2, HEAD_DIM) * comp_mask).reshape(bsz, 2 * H_A, 2 * HEAD_DIM)
        per_head = lambda a: jnp.repeat(a.reshape(bsz, H_A, 2 * HEAD_DIM), 2, axis=1)
        soa = _dec_a(l, page_table, pad_rows(q_rows), ck_a, cv_a, bias_da, pad_rows(per_head(ska)),
                     pad_rows(per_head(sva)), b0_a, lw["lam_vecs"], lw["subln"], lam_init)
        sob, s_new = _dec_ret(l, sqb, skb, svb, sgb, state_ret, gammas)
        idx = _dec_c1(l, page_table, sqc.reshape(bsz, W_C, 1), ck_c)
        h4 = lambda a: a.reshape(bsz, H_C, 1, HEAD_DIM)
        soc = _dec_c2(l, page_table, idx, h4(sqc_bf.astype(F32)), ck_c, cv_c, bias_dc, h4(skc), h4(svc), b0_c)
        xs = _merge(xs, lw["ln_mix"], lw["w_g"], soa.reshape(bsz, W_A).astype(BF16),
                    sob.reshape(bsz, W_BV).astype(BF16), soc.reshape(bsz, W_C).astype(BF16),
                    lw["w_up_a"], lw["w_up_b"], lw["w_up_c"], lw["w_o"])
        xs = _ffn(xs, *lw["ffn2"])
        rows_s.append((ska.reshape(bsz, 1, H_A, 2 * HEAD_DIM), sva.reshape(bsz, 1, H_A, 2 * HEAD_DIM),
                       skc.reshape(bsz, 1, H_C, HEAD_DIM), svc.reshape(bsz, 1, H_C, HEAD_DIM), s_new))

    stack = lambda rows, i: jnp.stack([r[i] for r in rows])
    return (xp[None], xs[:, None, :],
            stack(rows_p, 0), stack(rows_p, 1), stack(rows_p, 2), stack(rows_p, 3), stack(rows_p, 4),
            stack(rows_s, 0), stack(rows_s, 1), stack(rows_s, 2), stack(rows_s, 3), stack(rows_s, 4))
```

```python
import functools
import math

import jax
import jax.numpy as jnp
import numpy as np
from jax import lax
from jax.experimental import pallas as pl
from jax.experimental.pallas import tpu as pltpu

F32 = jnp.float32
BF16 = jnp.bfloat16

D_MODEL = 1024
D_FF = 2816
HEAD_DIM = 64
H_A = 4
H_B = 4
DK_B = 64
DV_B = 128
H_C = 8
RET_CHUNK = 128
MOBA_BLOCK = 256
MOBA_TOPK = 3
N_BUCKETS = 32
MAX_DISTANCE = 128
NORM_EPS = 1e-6
NEG = -1e30
W_A = H_A * 2 * HEAD_DIM
W_BQK = H_B * DK_B
W_BV = H_B * DV_B
W_C = H_C * HEAD_DIM
QK_SCALE = HEAD_DIM ** -0.5
LOG2E = math.log2(math.e)

LANES = 128
VMEM_LIMIT = 56 * 1024 * 1024
ROW_TILE = 512
ATT_A_TQ = 1024
ATT_A_TK = 512
ATT_C_TQ = 512
PAGES_PER_STEP = 16


def _cparams(sem):
    return pltpu.CompilerParams(dimension_semantics=sem, vmem_limit_bytes=VMEM_LIMIT)


def _rms_rows(x, g):
    ms = jnp.mean(x * x, axis=-1, keepdims=True)
    return x * lax.rsqrt(ms + NORM_EPS) * g


def _dot_nt(a, b, **kw):
    return lax.dot_general(a, b, (((1,), (1,)), ((), ())), preferred_element_type=F32, **kw)


def _t5_thresholds():
    n = np.arange(0, 4 * MAX_DISTANCE)
    max_exact = N_BUCKETS // 2
    nf = np.maximum(n, 1).astype(np.float32)
    large = max_exact + (np.log(nf / np.float32(max_exact)) / np.float32(math.log(MAX_DISTANCE / max_exact))
                         * np.float32(N_BUCKETS - max_exact)).astype(np.int32)
    bucket = np.where(n < max_exact, n, np.minimum(large, N_BUCKETS - 1))
    return [int(np.argmax(bucket >= t)) for t in range(N_BUCKETS)]


T5_THR = _t5_thresholds()
T5_FAR = T5_THR[-1]


def _ffn_body(x_ref, g_ref, wg_ref, wu_ref, wd_ref, o_ref, h_scr, acc_scr):
    j = pl.program_id(1)

    @pl.when(j == 0)
    def _():
        h_scr[...] = _rms_rows(x_ref[...], g_ref[...]).astype(BF16)
        acc_scr[...] = jnp.zeros_like(acc_scr)

    h = h_scr[...]
    a = jnp.dot(h, wg_ref[...], preferred_element_type=F32)
    b = jnp.dot(h, wu_ref[...], preferred_element_type=F32)
    z = a * jax.nn.sigmoid(a) * b
    acc_scr[...] += jnp.dot(z.astype(BF16), wd_ref[...], preferred_element_type=F32)

    @pl.when(j == pl.num_programs(1) - 1)
    def _():
        o_ref[...] = x_ref[...] + 0.5 * acc_scr[...]


def _ffn(x, g, wg, wu, wd):
    m, d = x.shape
    f = wg.shape[1]
    tm = min(ROW_TILE, m)
    tf = f // 2
    assert m % tm == 0 and f % tf == 0 and tf % LANES == 0
    return pl.pallas_call(
        _ffn_body,
        grid=(m // tm, f // tf),
        in_specs=[
            pl.BlockSpec((tm, d), lambda i, j: (i, 0)),
            pl.BlockSpec((1, d), lambda i, j: (0, 0)),
            pl.BlockSpec((d, tf), lambda i, j: (0, j)),
            pl.BlockSpec((d, tf), lambda i, j: (0, j)),
            pl.BlockSpec((tf, d), lambda i, j: (j, 0)),
        ],
        out_specs=pl.BlockSpec((tm, d), lambda i, j: (i, 0)),
        out_shape=jax.ShapeDtypeStruct((m, d), F32),
        scratch_shapes=[pltpu.VMEM((tm, d), BF16), pltpu.VMEM((tm, d), F32)],
        compiler_params=_cparams(("parallel", "arbitrary")),
        name="ffn",
    )(x, g, wg, wu, wd)


def _proj_body(*refs, segs, n_norm, has_rot):
    it = iter(refs)
    x_ref, g_ref, w_ref = next(it), next(it), next(it)
    gains_ref = bd_ref = cos_ref = sin_ref = None
    if n_norm:
        gains_ref, bd_ref = next(it), next(it)
    if has_rot:
        cos_ref, sin_ref = next(it), next(it)
    outs = list(it)
    h = _rms_rows(x_ref[...], g_ref[...]).astype(BF16)
    off = oi = ni = 0
    for width, kind, scale_f32, scale_bf16, want_f32, want_bf16 in segs:
        y = jnp.dot(h, w_ref[:, off:off + width], preferred_element_type=F32)
        off += width
        if kind == "norm":
            y2 = y * y
            hi = y2.astype(BF16)
            lo = (y2 - hi.astype(F32)).astype(BF16)
            ss = (jnp.dot(hi, bd_ref[...], preferred_element_type=F32)
                  + jnp.dot(lo, bd_ref[...], preferred_element_type=F32))
            y = y * lax.rsqrt(ss * (1.0 / HEAD_DIM) + NORM_EPS) * gains_ref[ni:ni + 1, :]
            ni += 1
        elif kind == "rot":
            lane = lax.broadcasted_iota(jnp.int32, y.shape, 1)
            first_half = (lane % HEAD_DIM) < (HEAD_DIM // 2)
            swapped = jnp.where(first_half, pltpu.roll(y, width - HEAD_DIM // 2, 1),
                                pltpu.roll(y, HEAD_DIM // 2, 1))
            y = y * cos_ref[...] + swapped * sin_ref[...]
        if scale_f32 != 1.0:
            y = y * scale_f32
        if want_f32:
            outs[oi][...] = y
            oi += 1
        if want_bf16:
            outs[oi][...] = (y * scale_bf16).astype(BF16) if scale_bf16 != 1.0 else y.astype(BF16)
            oi += 1


def _proj(x, g, w, segs, gains=None, bd=None, cos=None, sin=None):
    m, d = x.shape
    tm = min(ROW_TILE, m)
    assert m % tm == 0
    wtot = w.shape[1]
    n_norm = sum(1 for s in segs if s[1] == "norm")
    has_rot = any(s[1] == "rot" for s in segs)
    args = [x, g, w]
    in_specs = [
        pl.BlockSpec((tm, d), lambda i: (i, 0)),
        pl.BlockSpec((1, d), lambda i: (0, 0)),
        pl.BlockSpec((d, wtot), lambda i: (0, 0)),
    ]
    if n_norm:
        args += [gains, bd]
        in_specs += [pl.BlockSpec(gains.shape, lambda i: (0, 0)), pl.BlockSpec(bd.shape, lambda i: (0, 0))]
    if has_rot:
        args += [cos, sin]
        in_specs += [pl.BlockSpec((tm, cos.shape[1]), lambda i: (i, 0)),
                     pl.BlockSpec((tm, sin.shape[1]), lambda i: (i, 0))]
    out_shape, out_specs = [], []
    for width, _, _, _, want_f32, want_bf16 in segs:
        for want, dt in ((want_f32, F32), (want_bf16, BF16)):
            if want:
                out_shape.append(jax.ShapeDtypeStruct((m, width), dt))
                out_specs.append(pl.BlockSpec((tm, width), lambda i: (i, 0)))
    return pl.pallas_call(
        functools.partial(_proj_body, segs=segs, n_norm=n_norm, has_rot=has_rot),
        grid=(m // tm,),
        in_specs=in_specs,
        out_specs=out_specs,
        out_shape=out_shape,
        compiler_params=_cparams(("parallel",)),
        name="proj",
    )(*args)


def _bias_tile_body(offs_ref, rb_ref, o_ref, *, rmul, head0, mult):
    k = pl.program_id(0)
    hd = head0 + pl.program_id(1)
    tq, tk = o_ref.shape
    r = lax.broadcasted_iota(jnp.int32, (tq, tk), 0) * rmul
    c = lax.broadcasted_iota(jnp.int32, (tq, tk), 1)
    rel = r - c + offs_ref[k]
    b_far = rb_ref[N_BUCKETS - 1, hd]
    val = jnp.full((tq, tk), (rb_ref[0, hd] - b_far) * mult, F32)
    for t in range(1, N_BUCKETS):
        val = jnp.where(rel >= T5_THR[t], (rb_ref[t, hd] - b_far) * mult, val)
    o_ref[...] = jnp.where(rel >= 0, val, NEG)


def _bias_tiles(rel_bias, offs, n_heads, head0, tq, tk, rmul, mult=1.0):
    offs = jnp.asarray(np.asarray(offs, np.int32))
    nk = offs.shape[0]
    return pl.pallas_call(
        functools.partial(_bias_tile_body, rmul=rmul, head0=head0, mult=mult),
        grid_spec=pltpu.PrefetchScalarGridSpec(
            num_scalar_prefetch=1,
            grid=(nk, n_heads),
            in_specs=[pl.BlockSpec(memory_space=pltpu.SMEM)],
            out_specs=pl.BlockSpec((None, None, tq, tk), lambda k, h, offs: (k, h, 0, 0)),
        ),
        out_shape=jax.ShapeDtypeStruct((nk, n_heads, tq, tk), F32),
        compiler_params=_cparams(("arbitrary", "arbitrary")),
        name="bias_tiles",
    )(offs, rel_bias)


def _tile_schedule(t, tq, tk):
    offs_kind = {}
    far_off = 1 << 24
    qi, kj, kind, flags = [], [], [], []
    for i in range(t // tq):
        js = [j for j in range(t // tk) if j * tk <= i * tq + tq - 1]
        for n, j in enumerate(js):
            off = i * tq - j * tk
            if off - (tk - 1) >= T5_FAR:
                off = far_off
            kd = offs_kind.setdefault(off, len(offs_kind))
            qi.append(i)
            kj.append(j)
            kind.append(kd)
            flags.append((1 if n == 0 else 0) | (2 if n == len(js) - 1 else 0))
    offs = [o for o, _ in sorted(offs_kind.items(), key=lambda kv: kv[1])]
    far_kind = offs_kind.get(far_off)
    for n in reversed(range(len(kind))):
        if kind[n] == far_kind:
            flags[n] |= 4
            kind[n] = kind[n + 1]
    as_i32 = lambda v: jnp.asarray(np.asarray(v, np.int32))
    return as_i32(qi), as_i32(kj), as_i32(kind), as_i32(flags), offs, far_kind is not None


def _lambda_value(lam_ref, lam_init):
    lv = lam_ref[...]
    s1 = jnp.sum(lv[0:1, :] * lv[1:2, :], axis=-1, keepdims=True)
    s2 = jnp.sum(lv[2:3, :] * lv[3:4, :], axis=-1, keepdims=True)
    return jnp.exp(s1) - jnp.exp(s2) + lam_init


def _sweep_by_kind(sweep, flag, has_far):
    if not has_far:
        sweep(True)
    else:
        pl.when((flag & 4) != 0)(functools.partial(sweep, False))
        pl.when((flag & 4) == 0)(functools.partial(sweep, True))


def _attn_a_body(qi_ref, kj_ref, kind_ref, fl_ref, q_ref, k_ref, v_ref, b_ref, lam_ref, sg_ref, o_ref,
                 m_scr, l_scr, acc_scr, *, lam_init, has_far):
    t = pl.program_id(0)
    flag = fl_ref[t]

    @pl.when((flag & 1) != 0)
    def _():
        m_scr[...] = jnp.full_like(m_scr, NEG)
        l_scr[...] = jnp.zeros_like(l_scr)
        acc_scr[...] = jnp.zeros_like(acc_scr)

    hw = 2 * HEAD_DIM
    reps = k_ref.shape[0] // LANES
    lane = lax.broadcasted_iota(jnp.int32, (q_ref.shape[0], hw), 1)

    def sweep(with_bias):
        for h in range(H_A):
            qh = q_ref[:, h * hw:(h + 1) * hw]
            kh = k_ref[:, h * hw:(h + 1) * hw]
            vh = v_ref[:, h * hw:(h + 1) * hw]
            for c in range(2):
                i = 2 * h + c
                comp = (lane < HEAD_DIM) if c == 0 else (lane >= HEAD_DIM)
                s = _dot_nt(jnp.where(comp, qh, jnp.zeros_like(qh)), kh)
                if with_bias:
                    s = s + b_ref[h]
                m_prev = m_scr[i]
                m_new = jnp.maximum(m_prev, jnp.max(s, axis=-1, keepdims=True))
                alpha = jnp.exp2(m_prev - m_new)
                p = jnp.exp2(s - jnp.concatenate([m_new] * reps, axis=1))
                l_scr[i] = alpha * l_scr[i] + jnp.sum(p, axis=-1, keepdims=True)
                acc_scr[i] = alpha * acc_scr[i] + jnp.dot(p.astype(BF16), vh, preferred_element_type=F32)
                m_scr[i] = m_new

    _sweep_by_kind(sweep, flag, has_far)

    @pl.when((flag & 2) != 0)
    def _():
        lam = _lambda_value(lam_ref, lam_init)
        for h in range(H_A):
            o = acc_scr[2 * h] / l_scr[2 * h] - lam * (acc_scr[2 * h + 1] / l_scr[2 * h + 1])
            o = _rms_rows(o, sg_ref[...]) * (1.0 - lam_init)
            o_ref[:, h * hw:(h + 1) * hw] = o.astype(BF16)


def _attn_a(q, k, v, bias, sched, lam_vecs, subln_g, lam_init):
    t = q.shape[0]
    tq, tk = min(ATT_A_TQ, t), min(ATT_A_TK, t)
    qi, kj, kind, flags, _, has_far = sched
    n = qi.shape[0]
    return pl.pallas_call(
        functools.partial(_attn_a_body, lam_init=lam_init, has_far=has_far),
        grid_spec=pltpu.PrefetchScalarGridSpec(
            num_scalar_prefetch=4,
            grid=(n,),
            in_specs=[
                pl.BlockSpec((tq, W_A), lambda s, qi, kj, kd, fl: (qi[s], 0)),
                pl.BlockSpec((tk, W_A), lambda s, qi, kj, kd, fl: (kj[s], 0)),
                pl.BlockSpec((tk, W_A), lambda s, qi, kj, kd, fl: (kj[s], 0)),
                pl.BlockSpec((None, H_A, tq, tk), lambda s, qi, kj, kd, fl: (kd[s], 0, 0, 0)),
                pl.BlockSpec((4, HEAD_DIM), lambda s, qi, kj, kd, fl: (0, 0)),
                pl.BlockSpec((1, 2 * HEAD_DIM), lambda s, qi, kj, kd, fl: (0, 0)),
            ],
            out_specs=pl.BlockSpec((tq, W_A), lambda s, qi, kj, kd, fl: (qi[s], 0)),
            scratch_shapes=[pltpu.VMEM((2 * H_A, tq, LANES), F32), pltpu.VMEM((2 * H_A, tq, LANES), F32),
                            pltpu.VMEM((2 * H_A, tq, 2 * HEAD_DIM), F32)],
        ),
        out_shape=jax.ShapeDtypeStruct((t, W_A), BF16),
        compiler_params=_cparams(("arbitrary",)),
        name="attn_a",
    )(qi, kj, kind, flags, q, k, v, bias, lam_vecs, subln_g)


def _ret_constants(chunk):
    log_g = np.log(1.0 - 2.0 ** (-5.0 - np.arange(H_B, dtype=np.float64)))
    i = np.arange(chunk, dtype=np.float64)
    d = i[:, None] - i[None, :]
    decay = np.where(d >= 0, np.exp(np.maximum(d, 0.0)[None] * log_g[:, None, None]), 0.0)
    head_of_lane = np.arange(W_BQK) // DK_B
    q_w = np.exp((i[:, None] + 1.0) * log_g[head_of_lane][None, :])
    k_w_t = np.exp((chunk - 1 - i)[None, :] * log_g[head_of_lane][:, None])
    g_c = np.exp(chunk * log_g)[head_of_lane]
    g_c = np.broadcast_to(g_c.reshape(H_B // 2, 2 * DK_B, 1), (H_B // 2, 2 * DK_B, DV_B))
    f = lambda a: jnp.asarray(np.asarray(a, np.float32))
    return f(decay), f(q_w), f(k_w_t), f(g_c), np.exp(log_g)


def _ret_body(q_ref, k_ref, kt_ref, v_ref, gb_ref, s0_ref, dec_ref, qw_ref, kwt_ref, gc_ref,
              o_ref, sfin_ref, s_scr):
    n = pl.program_id(0)

    @pl.when(n == 0)
    def _():
        s_scr[...] = s0_ref[...]

    pw = 2 * DK_B
    lane = lax.broadcasted_iota(jnp.int32, (q_ref.shape[0], pw), 1)
    row = lax.broadcasted_iota(jnp.int32, (pw, kt_ref.shape[1]), 0)
    for g in range(H_B // 2):
        s_prev = s_scr[g]
        s_prev_bf = s_prev.astype(BF16)
        qp = q_ref[:, g * pw:(g + 1) * pw]
        kp = k_ref[:, g * pw:(g + 1) * pw].astype(BF16)
        qwp = qp * qw_ref[:, g * pw:(g + 1) * pw]
        ktw = kt_ref[g * pw:(g + 1) * pw, :] * kwt_ref[g * pw:(g + 1) * pw, :]
        upd = jnp.zeros_like(s_prev)
        for hh in range(2):
            h = 2 * g + hh
            in_head = (lane < DK_B) if hh == 0 else (lane >= DK_B)
            in_head_r = (row < DK_B) if hh == 0 else (row >= DK_B)
            vh = v_ref[:, h * DV_B:(h + 1) * DV_B].astype(BF16)
            sc = _dot_nt(jnp.where(in_head, qp, 0.0).astype(BF16), kp) * dec_ref[h]
            o = (jnp.dot(sc.astype(BF16), vh, preferred_element_type=F32)
                 + jnp.dot(jnp.where(in_head, qwp, 0.0).astype(BF16), s_prev_bf, preferred_element_type=F32))
            o = o * lax.rsqrt(jnp.mean(o * o, axis=-1, keepdims=True) + NORM_EPS)
            gate = gb_ref[:, h * DV_B:(h + 1) * DV_B]
            o_ref[:, h * DV_B:(h + 1) * DV_B] = (o * (gate * jax.nn.sigmoid(gate))).astype(BF16)
            upd = upd + jnp.dot(jnp.where(in_head_r, ktw, 0.0).astype(BF16), vh, preferred_element_type=F32)
        s_scr[g] = gc_ref[g] * s_prev + upd

    @pl.when(n == pl.num_programs(0) - 1)
    def _():
        sfin_ref[...] = s_scr[...]


def _retention(q, k, v, gb, s0_pairs, consts):
    t = q.shape[0]
    c = RET_CHUNK
    decay, q_w, k_w_t, g_c, _ = consts
    kt = k.T
    full = lambda a: pl.BlockSpec(a.shape, lambda n: (0,) * a.ndim)
    return pl.pallas_call(
        _ret_body,
        grid=(t // c,),
        in_specs=[
            pl.BlockSpec((c, W_BQK), lambda n: (n, 0)),
            pl.BlockSpec((c, W_BQK), lambda n: (n, 0)),
            pl.BlockSpec((W_BQK, c), lambda n: (0, n)),
            pl.BlockSpec((c, W_BV), lambda n: (n, 0)),
            pl.BlockSpec((c, W_BV), lambda n: (n, 0)),
            full(s0_pairs), full(decay), full(q_w), full(k_w_t), full(g_c),
        ],
        out_specs=[pl.BlockSpec((c, W_BV), lambda n: (n, 0)), full(s0_pairs)],
        out_shape=[jax.ShapeDtypeStruct((t, W_BV), BF16), jax.ShapeDtypeStruct(s0_pairs.shape, F32)],
        scratch_shapes=[pltpu.VMEM(s0_pairs.shape, F32)],
        compiler_params=_cparams(("arbitrary",)),
        name="retention",
    )(q, k, kt, v, gb, s0_pairs, decay, q_w, k_w_t, g_c)


def _kmean_body(k_ref, o_ref):
    nb = o_ref.shape[0]
    kk = k_ref[...].reshape(nb, MOBA_BLOCK, k_ref.shape[1])
    o_ref[...] = jnp.sum(kk, axis=1) * (1.0 / MOBA_BLOCK)


def _kmean(k):
    t, w = k.shape
    nblk = t // MOBA_BLOCK
    nb = 8
    assert nblk % nb == 0
    return pl.pallas_call(
        _kmean_body,
        grid=(nblk // nb,),
        in_specs=[pl.BlockSpec((nb * MOBA_BLOCK, w), lambda i: (i, 0))],
        out_specs=pl.BlockSpec((nb, w), lambda i: (i, 0)),
        out_shape=jax.ShapeDtypeStruct((nblk, w), F32),
        compiler_params=_cparams(("parallel",)),
        name="moba_kmean",
    )(k)


def _top_blocks(gate, valid):
    lane = lax.broadcasted_iota(jnp.int32, gate.shape, 1).astype(F32)
    g = jnp.where(valid, gate, -jnp.inf)
    picked = jnp.zeros(gate.shape, F32)
    idxs = []
    for _ in range(MOBA_TOPK):
        m = jnp.max(g, axis=-1, keepdims=True)
        cand = jnp.logical_and(g == m, m > -jnp.inf)
        idx = jnp.min(jnp.where(cand, lane, float(LANES)), axis=-1, keepdims=True)
        pick = lane == idx
        picked = jnp.where(pick, 1.0, picked)
        g = jnp.where(pick, -jnp.inf, g)
        idxs.append(idx)
    return picked, idxs


def _select_body(q_ref, km_ref, o_ref, *, pos0):
    tq = q_ref.shape[0]
    q = q_ref[...]
    km = km_ref[...]
    pw = 2 * HEAD_DIM
    blk = lax.broadcasted_iota(jnp.int32, (tq, LANES), 1)
    qpos = pos0 + pl.program_id(0) * tq + lax.broadcasted_iota(jnp.int32, (tq, LANES), 0)
    past = blk < qpos // MOBA_BLOCK
    for h in range(H_C):
        g, hh = h // 2, h % 2
        in_head = (blk < HEAD_DIM) if hh == 0 else (blk >= HEAD_DIM)
        qp = q[:, g * pw:(g + 1) * pw]
        gate = _dot_nt(jnp.where(in_head, qp, 0.0), km[:, g * pw:(g + 1) * pw], precision=lax.Precision.HIGHEST)
        picked, _ = _top_blocks(gate, past)
        sel = jnp.where(jnp.logical_and(past, picked == 0.0), NEG, 0.0)
        if hh == 0:
            sel = pltpu.roll(sel, HEAD_DIM, 1)
        qa = jnp.where(in_head, qp * (QK_SCALE * LOG2E), sel)
        o_ref[:, h * LANES:(h + 1) * LANES] = qa.astype(BF16)


def _select(q, kmean_pad, pos0):
    t, w = q.shape
    tq = min(ROW_TILE, t)
    assert (pos0 + t) // MOBA_BLOCK <= HEAD_DIM
    return pl.pallas_call(
        functools.partial(_select_body, pos0=pos0),
        grid=(t // tq,),
        in_specs=[pl.BlockSpec((tq, w), lambda i: (i, 0)),
                  pl.BlockSpec(kmean_pad.shape, lambda i: (0, 0))],
        out_specs=pl.BlockSpec((tq, H_C * LANES), lambda i: (i, 0)),
        out_shape=jax.ShapeDtypeStruct((t, H_C * LANES), BF16),
        compiler_params=_cparams(("parallel",)),
        name="moba_select",
    )(q, kmean_pad)


def _attn_c_body(qi_ref, kj_ref, kind_ref, fl_ref, q_ref, k_ref, v_ref, b_ref, o_ref, m_scr, acc_scr, *, has_far):
    t = pl.program_id(0)
    flag = fl_ref[t]
    j = kj_ref[t]

    @pl.when((flag & 1) != 0)
    def _():
        m_scr[...] = jnp.full_like(m_scr, NEG)
        acc_scr[...] = jnp.zeros_like(acc_scr)

    pw = 2 * HEAD_DIM
    tk = k_ref.shape[0]
    reps = tk // LANES
    lane_k = lax.broadcasted_iota(jnp.int32, (tk, pw), 1)

    def sweep(with_bias):
        for h in range(H_C):
            g, hh = h // 2, h % 2
            kp = k_ref[:, g * pw:(g + 1) * pw]
            vp = v_ref[:, g * pw:(g + 1) * pw]
            in_head = (lane_k < HEAD_DIM) if hh == 0 else (lane_k >= HEAD_DIM)
            onehot = lane_k == j + (HEAD_DIM if hh == 0 else 0)
            ka = jnp.where(in_head, kp, jnp.where(onehot, 1.0, 0.0).astype(BF16))
            va = jnp.where(in_head, vp, jnp.ones_like(vp))
            s = _dot_nt(q_ref[:, h * LANES:(h + 1) * LANES], ka)
            if with_bias:
                s = s + b_ref[h]
            m_prev = m_scr[h]
            m_new = jnp.maximum(m_prev, jnp.max(s, axis=-1, keepdims=True))
            alpha = jnp.exp2(m_prev - m_new)
            p = jnp.exp2(s - jnp.concatenate([m_new] * reps, axis=1))
            acc_scr[h] = alpha * acc_scr[h] + jnp.dot(p.astype(BF16), va, preferred_element_type=F32)
            m_scr[h] = m_new

    _sweep_by_kind(sweep, flag, has_far)

    @pl.when((flag & 2) != 0)
    def _():
        lane = lax.broadcasted_iota(jnp.int32, (q_ref.shape[0], pw), 1)
        for g in range(H_C // 2):
            a0, a1 = acc_scr[2 * g], acc_scr[2 * g + 1]
            o0 = a0 / pltpu.roll(a0, HEAD_DIM, 1)
            o1 = a1 / pltpu.roll(a1, HEAD_DIM, 1)
            o_ref[:, g * pw:(g + 1) * pw] = jnp.where(lane < HEAD_DIM, o0, o1).astype(BF16)


def _attn_c(q_aug, k, v, bias, sched):
    t = q_aug.shape[0]
    tq = min(ATT_C_TQ, t)
    tk = MOBA_BLOCK
    qi, kj, kind, flags, _, has_far = sched
    n = qi.shape[0]
    return pl.pallas_call(
        functools.partial(_attn_c_body, has_far=has_far),
        grid_spec=pltpu.PrefetchScalarGridSpec(
            num_scalar_prefetch=4,
            grid=(n,),
            in_specs=[
                pl.BlockSpec((tq, H_C * LANES), lambda s, qi, kj, kd, fl: (qi[s], 0)),
                pl.BlockSpec((tk, W_C), lambda s, qi, kj, kd, fl: (kj[s], 0)),
                pl.BlockSpec((tk, W_C), lambda s, qi, kj, kd, fl: (kj[s], 0)),
                pl.BlockSpec((None, H_C, tq, tk), lambda s, qi, kj, kd, fl: (kd[s], 0, 0, 0)),
            ],
            out_specs=pl.BlockSpec((tq, W_C), lambda s, qi, kj, kd, fl: (qi[s], 0)),
            scratch_shapes=[pltpu.VMEM((H_C, tq, LANES), F32), pltpu.VMEM((H_C, tq, 2 * HEAD_DIM), F32)],
        ),
        out_shape=jax.ShapeDtypeStruct((t, W_C), BF16),
        compiler_params=_cparams(("arbitrary",)),
        name="attn_c",
    )(qi, kj, kind, flags, q_aug, k, v, bias)


def _merge_body(x_ref, g_ref, wg_ref, oa_ref, ob_ref, oc_ref, wa_ref, wb_ref, wc_ref, wo_ref, o_ref):
    x = x_ref[...]
    h = _rms_rows(x, g_ref[...]).astype(BF16)
    mix = None
    for n, (o_in, w_up) in enumerate(((oa_ref, wa_ref), (ob_ref, wb_ref), (oc_ref, wc_ref))):
        y = jnp.dot(o_in[...], w_up[...], preferred_element_type=F32)
        gl = jnp.dot(h, wg_ref[:, n * D_MODEL:(n + 1) * D_MODEL], preferred_element_type=F32)
        term = jax.nn.sigmoid(gl) * y
        mix = term if mix is None else mix + term
    o_ref[...] = x + jnp.dot(mix.astype(BF16), wo_ref[...], preferred_element_type=F32)


def _merge(x, g, wg, oa, ob, oc, wa, wb, wc, wo):
    m, d = x.shape
    tm = min(ROW_TILE, m)
    rows = lambda w: pl.BlockSpec((tm, w), lambda i: (i, 0))
    full = lambda a: pl.BlockSpec(a.shape, lambda i: (0, 0))
    return pl.pallas_call(
        _merge_body,
        grid=(m // tm,),
        in_specs=[rows(d), full(g), full(wg), rows(W_A), rows(W_BV), rows(W_C),
                  full(wa), full(wb), full(wc), full(wo)],
        out_specs=rows(d),
        out_shape=jax.ShapeDtypeStruct((m, d), F32),
        compiler_params=_cparams(("parallel",)),
        name="merge",
    )(x, g, wg, oa, ob, oc, wa, wb, wc, wo)


DEC_ROWS = 16


def _dec_a_body(pt_ref, q_ref, *refs, npg, lam_init):
    k_refs, v_refs = refs[:npg], refs[npg:2 * npg]
    b_ref, knew_ref, vnew_ref, b0_ref, lam_ref, sg_ref, o_ref, m_scr, l_scr, acc_scr = refs[2 * npg:]
    p = pl.program_id(1)
    prow = k_refs[0].shape[0]

    @pl.when(p == 0)
    def _():
        m_scr[...] = jnp.full_like(m_scr, NEG)
        l_scr[...] = jnp.zeros_like(l_scr)
        acc_scr[...] = jnp.zeros_like(acc_scr)

    qrows = q_ref[...].astype(BF16)
    s = jnp.concatenate([_dot_nt(qrows, k_refs[i][...].astype(BF16)) for i in range(npg)], axis=1) + b_ref[...]
    m_prev = m_scr[...]
    m_new = jnp.maximum(m_prev, jnp.max(s, axis=-1, keepdims=True))
    alpha = jnp.exp(m_prev - m_new)
    pr = jnp.exp(s - m_new).astype(BF16)
    l_scr[...] = alpha * l_scr[...] + jnp.sum(pr.astype(F32), axis=-1, keepdims=True)
    acc = alpha * acc_scr[...]
    for i in range(npg):
        acc = acc + jnp.dot(pr[:, i * prow:(i + 1) * prow], v_refs[i][...].astype(BF16),
                            preferred_element_type=F32)
    acc_scr[...] = acc
    m_scr[...] = m_new

    @pl.when(p == pl.num_programs(1) - 1)
    def _():
        k_new = knew_ref[...].astype(BF16).astype(F32)
        v_new = vnew_ref[...].astype(BF16).astype(F32)
        s_new = jnp.sum(qrows.astype(F32) * k_new, axis=-1, keepdims=True) + b0_ref[...]
        m_fin = jnp.maximum(m_scr[...], s_new)
        a_fin = jnp.exp(m_scr[...] - m_fin)
        p_new = jnp.exp(s_new - m_fin).astype(BF16).astype(F32)
        l_fin = a_fin * l_scr[...] + p_new
        o = (a_fin * acc_scr[...] + p_new * v_new) / l_fin
        lam = _lambda_value(lam_ref, lam_init)
        hw = 2 * HEAD_DIM
        for h in range(H_A):
            oh = o[2 * h:2 * h + 1, :] - lam * o[2 * h + 1:2 * h + 2, :]
            o_ref[:, h * hw:(h + 1) * hw] = _rms_rows(oh, sg_ref[...]) * (1.0 - lam_init)


def _dec_a(layer, page_table, qrows, cache_k, cache_v, bias, k_new, v_new, b0, lam_vecs, subln_g, lam_init):
    bsz, n_pages = page_table.shape
    prow = cache_k.shape[2]
    npg = min(PAGES_PER_STEP, n_pages)
    assert n_pages % npg == 0
    n_steps = n_pages // npg
    hw = 2 * HEAD_DIM
    seq_rows = pl.BlockSpec((None, DEC_ROWS, hw), lambda b, p, pt: (b, 0, 0))
    const = lambda a: pl.BlockSpec(a.shape, lambda b, p, pt: (0,) * a.ndim)

    def page_spec(i):
        return pl.BlockSpec((None, None, prow, hw), lambda b, p, pt: (layer, pt[b, p * npg + i], 0, 0))

    in_specs = ([seq_rows] + [page_spec(i) for i in range(npg)] + [page_spec(i) for i in range(npg)]
                + [pl.BlockSpec((None, DEC_ROWS, npg * prow), lambda b, p, pt: (p // (n_steps - 1) if n_steps > 1 else 1, 0, 0)),
                   seq_rows, seq_rows, const(b0), const(lam_vecs), const(subln_g)])
    return pl.pallas_call(
        functools.partial(_dec_a_body, npg=npg, lam_init=lam_init),
        grid_spec=pltpu.PrefetchScalarGridSpec(
            num_scalar_prefetch=1,
            grid=(bsz, n_steps),
            in_specs=in_specs,
            out_specs=pl.BlockSpec((None, 1, W_A), lambda b, p, pt: (b, 0, 0)),
            scratch_shapes=[pltpu.VMEM((DEC_ROWS, 1), F32), pltpu.VMEM((DEC_ROWS, 1), F32),
                            pltpu.VMEM((DEC_ROWS, hw), F32)],
        ),
        out_shape=jax.ShapeDtypeStruct((bsz, 1, W_A), F32),
        compiler_params=_cparams(("parallel", "arbitrary")),
        name="decode_attn_a",
    )(page_table, qrows, *([cache_k] * npg), *([cache_v] * npg), bias, k_new, v_new, b0, lam_vecs, subln_g)


def _dec_ret_body(qc_ref, kc_ref, v_ref, gb_ref, s0_ref, o_ref, s_ref, *, gammas):
    for h in range(H_B):
        vh = v_ref[:, h * DV_B:(h + 1) * DV_B]
        s_new = float(gammas[h]) * s0_ref[h] + kc_ref[h] * vh
        s_ref[h] = s_new
        o = jnp.sum(qc_ref[h] * s_new, axis=0, keepdims=True)
        o = o * lax.rsqrt(jnp.mean(o * o, axis=-1, keepdims=True) + NORM_EPS)
        gate = gb_ref[:, h * DV_B:(h + 1) * DV_B]
        o_ref[:, h * DV_B:(h + 1) * DV_B] = o * (gate * jax.nn.sigmoid(gate))


def _dec_ret(layer, q, k, v, gb, state, gammas):
    bsz = q.shape[0]
    qc = q.reshape(bsz, H_B, DK_B, 1)
    kc = k.reshape(bsz, H_B, DK_B, 1)
    col = pl.BlockSpec((None, H_B, DK_B, 1), lambda b: (b, 0, 0, 0))
    row = pl.BlockSpec((None, 1, W_BV), lambda b: (b, 0, 0))
    return pl.pallas_call(
        functools.partial(_dec_ret_body, gammas=gammas),
        grid=(bsz,),
        in_specs=[col, col, row, row,
                  pl.BlockSpec((None, None, H_B, DK_B, DV_B), lambda b: (layer, b, 0, 0, 0))],
        out_specs=[row, pl.BlockSpec((None, H_B, DK_B, DV_B), lambda b: (b, 0, 0, 0))],
        out_shape=[jax.ShapeDtypeStruct((bsz, 1, W_BV), F32),
                   jax.ShapeDtypeStruct((bsz, H_B, DK_B, DV_B), F32)],
        compiler_params=_cparams(("parallel",)),
        name="decode_retention",
    )(qc, kc, v.reshape(bsz, 1, W_BV), gb.reshape(bsz, 1, W_BV), state)


def _dec_c1_body(pt_ref, q_ref, *refs, npg, ppb, n_past):
    k_refs = refs[:npg]
    o_ref, km_scr = refs[npg:]
    p = pl.program_id(1)

    @pl.when(p == 0)
    def _():
        km_scr[...] = jnp.zeros_like(km_scr)

    blk_lane = lax.broadcasted_iota(jnp.int32, km_scr.shape, 1)
    km = km_scr[...]
    for i in range(npg // ppb):
        pages = k_refs[i * ppb][...]
        for u in range(1, ppb):
            pages = pages + k_refs[i * ppb + u][...]
        tot = jnp.sum(pages, axis=-1, keepdims=True)
        km = jnp.where(blk_lane == p * (npg // ppb) + i, tot * (1.0 / MOBA_BLOCK), km)
    km_scr[...] = km

    @pl.when(p == pl.num_programs(1) - 1)
    def _():
        prod = km_scr[...] * q_ref[...]
        gate = jnp.sum(prod.reshape(H_C, HEAD_DIM, LANES), axis=1)
        blk = lax.broadcasted_iota(jnp.int32, gate.shape, 1)
        _, idxs = _top_blocks(gate, blk < n_past)
        out = jnp.full(gate.shape, -1, jnp.int32)
        for r, idx in enumerate(idxs):
            out = jnp.where(jnp.logical_and(blk == r, idx < float(LANES)), idx.astype(jnp.int32), out)
        o_ref[...] = out


def _dec_c1(layer, page_table, qcol, cache_kt):
    bsz, n_pages = page_table.shape
    page = cache_kt.shape[3]
    ppb = MOBA_BLOCK // page
    npg = min(PAGES_PER_STEP, n_pages)
    assert MOBA_BLOCK % page == 0 and npg % ppb == 0 and n_pages % npg == 0
    n_past = n_pages // ppb
    assert n_past <= LANES

    def page_spec(i):
        return pl.BlockSpec((None, None, W_C, page), lambda b, p, pt: (layer, pt[b, p * npg + i], 0, 0))

    return pl.pallas_call(
        functools.partial(_dec_c1_body, npg=npg, ppb=ppb, n_past=n_past),
        grid_spec=pltpu.PrefetchScalarGridSpec(
            num_scalar_prefetch=1,
            grid=(bsz, n_pages // npg),
            in_specs=[pl.BlockSpec((None, W_C, 1), lambda b, p, pt: (b, 0, 0))] + [page_spec(i) for i in range(npg)],
            out_specs=pl.BlockSpec((None, H_C, LANES), lambda b, p, pt: (b, 0, 0)),
            scratch_shapes=[pltpu.VMEM((W_C, LANES), F32)],
        ),
        out_shape=jax.ShapeDtypeStruct((bsz, H_C, LANES), jnp.int32),
        compiler_params=_cparams(("parallel", "arbitrary")),
        name="decode_moba_select",
    )(page_table, qcol, *([cache_kt] * npg))


def _dec_c2_body(pt_ref, idx_ref, q_ref, *refs, ppb):
    nk = MOBA_TOPK * ppb
    k_refs, v_refs, b_refs = refs[:nk], refs[nk:2 * nk], refs[2 * nk:2 * nk + MOBA_TOPK]
    knew_ref, vnew_ref, b0_ref, o_ref = refs[2 * nk + MOBA_TOPK:]
    b, h = pl.program_id(0), pl.program_id(1)
    q = q_ref[...]
    q_rows = jnp.broadcast_to(q, (DEC_ROWS, HEAD_DIM)).astype(BF16)
    ss = []
    for r in range(MOBA_TOPK):
        valid = idx_ref[(b * H_C + h) * MOBA_TOPK + r] >= 0
        s = jnp.concatenate([jnp.dot(q_rows, k_refs[r * ppb + u][...].astype(BF16), preferred_element_type=F32)
                             for u in range(ppb)], axis=1) + b_refs[r][...]
        ss.append(jnp.where(valid, s, NEG))
    k_new = knew_ref[...].astype(BF16).astype(F32)
    v_new = vnew_ref[...].astype(BF16).astype(F32)
    s_new = jnp.sum(q.astype(BF16).astype(F32) * k_new, axis=-1, keepdims=True) + b0_ref[...][:, 0:1]
    m = s_new
    for s in ss:
        m = jnp.maximum(m, jnp.max(s, axis=-1, keepdims=True))
    p_new = jnp.exp(s_new - m).astype(BF16).astype(F32)
    l = p_new
    o = p_new * v_new
    page = k_refs[0].shape[1]
    for r, s in enumerate(ss):
        p = jnp.exp(s - m).astype(BF16)
        l = l + jnp.sum(p.astype(F32), axis=-1, keepdims=True)
        for u in range(ppb):
            o = o + _dot_nt(p[:, u * page:(u + 1) * page], v_refs[r * ppb + u][...].astype(BF16))
    o_ref[...] = (o / l)[0:1, :]


def _dec_c2(layer, page_table, idx, q, cache_kt, cache_vt, bias_blocks, k_new, v_new, b0):
    bsz, n_pages = page_table.shape
    page = cache_kt.shape[3]
    ppb = MOBA_BLOCK // page
    idx_flat = idx[:, :, :MOBA_TOPK].reshape(-1)

    def blk_of(b, h, r, ix):
        return jnp.maximum(ix[(b * H_C + h) * MOBA_TOPK + r], 0)

    def page_spec(r, u):
        return pl.BlockSpec((None, None, HEAD_DIM, page),
                            lambda b, h, pt, ix: (layer, pt[b, blk_of(b, h, r, ix) * ppb + u], h, 0))

    def bias_spec(r):
        return pl.BlockSpec((None, None, 1, MOBA_BLOCK), lambda b, h, pt, ix: (blk_of(b, h, r, ix), h, 0, 0))

    seq_head = pl.BlockSpec((None, None, 1, HEAD_DIM), lambda b, h, pt, ix: (b, h, 0, 0))
    pages = [page_spec(r, u) for r in range(MOBA_TOPK) for u in range(ppb)]
    return pl.pallas_call(
        functools.partial(_dec_c2_body, ppb=ppb),
        grid_spec=pltpu.PrefetchScalarGridSpec(
            num_scalar_prefetch=2,
            grid=(bsz, H_C),
            in_specs=([seq_head] + pages + pages + [bias_spec(r) for r in range(MOBA_TOPK)]
                      + [seq_head, seq_head, pl.BlockSpec((None, 1, HEAD_DIM), lambda b, h, pt, ix: (h, 0, 0))]),
            out_specs=seq_head,
        ),
        out_shape=jax.ShapeDtypeStruct((bsz, H_C, 1, HEAD_DIM), F32),
        compiler_params=_cparams(("parallel", "arbitrary")),
        name="decode_moba_attn",
    )(page_table, idx_flat, q, *([cache_kt] * (MOBA_TOPK * ppb)), *([cache_vt] * (MOBA_TOPK * ppb)),
      *([bias_blocks] * MOBA_TOPK), k_new, v_new, b0)


def _rotary_tables(pos):
    half = HEAD_DIM // 2
    inv = 1.0 / (10000.0 ** (jnp.arange(half, dtype=F32) / half))
    ang = pos.astype(F32)[:, None] * inv[None, :]
    cos, sin = jnp.cos(ang), jnp.sin(ang)
    cos_t = jnp.tile(jnp.concatenate([cos, cos], axis=1), (1, H_B))
    sin_t = jnp.tile(jnp.concatenate([-sin, sin], axis=1), (1, H_B))
    return cos_t, sin_t


def _block_diag_ones():
    i = np.arange(W_A) // HEAD_DIM
    return jnp.asarray((i[:, None] == i[None, :]).astype(np.float32), dtype=BF16)


SEG_A = ((W_A, "norm", 1.0, QK_SCALE, False, True),
         (W_A, "norm", 1.0, 1.0, True, True),
         (W_A, "plain", 1.0, 1.0, True, True))
SEG_A_PROMPT = ((W_A, "norm", 1.0, QK_SCALE * LOG2E, False, True),) + SEG_A[1:]
SEG_B = ((W_BQK, "rot", 1.0, 1.0, True, False),
         (W_BQK, "rot", DK_B ** -0.5, 1.0, True, False),
         (W_BV, "plain", 1.0, 1.0, True, False),
         (W_BV, "plain", 1.0, 1.0, True, False))
SEG_C = ((W_C, "norm", 1.0, QK_SCALE, True, True),
         (W_C, "norm", 1.0, 1.0, True, True),
         (W_C, "plain", 1.0, 1.0, True, True))
SEG_C_PROMPT = ((W_C, "norm", 1.0, 1.0, True, False),) + SEG_C[1:]


def _layer_weights(l, w):
    bf = lambda a: a.astype(BF16)
    o_b, o_c, o_g = 3 * W_A, 3 * W_A + 2 * W_BQK + 2 * W_BV, 3 * W_A + 2 * W_BQK + 2 * W_BV + 3 * W_C
    w_in = w["w_in"][l]
    tile = lambda g: jnp.tile(g, W_A // HEAD_DIM)[None, :]
    return dict(
        ffn1=(w["ln_ffn1_g"][l][None, :], bf(w["ffn1_w_gate"][l]), bf(w["ffn1_w_up"][l]), bf(w["ffn1_w_down"][l])),
        ffn2=(w["ln_ffn2_g"][l][None, :], bf(w["ffn2_w_gate"][l]), bf(w["ffn2_w_up"][l]), bf(w["ffn2_w_down"][l])),
        ln_mix=w["ln_mix_g"][l][None, :],
        w_a=bf(w_in[:, :o_b]), w_b=bf(w_in[:, o_b:o_c]), w_c=bf(w_in[:, o_c:o_g]), w_g=bf(w_in[:, o_g:]),
        gains_a=jnp.concatenate([tile(w["qn_a_g"][l]), tile(w["kn_a_g"][l])], axis=0),
        gains_c=jnp.concatenate([tile(w["qn_c_g"][l]), tile(w["kn_c_g"][l])], axis=0),
        lam_vecs=jnp.stack([w["lam_q1"][l], w["lam_k1"][l], w["lam_q2"][l], w["lam_k2"][l]]),
        subln=w["subln_a_g"][l][None, :],
        w_up_a=bf(w["w_up_a"][l]), w_up_b=bf(w["w_up_b"][l]), w_up_c=bf(w["w_up_c"][l]), w_o=bf(w["w_o"][l]),
    )


def kernel(x_prompt, x_sample, cache_a_k, cache_a_v, cache_c_k, cache_c_v, state_ret, page_table, ln_ffn1_g, ffn1_w_gate, ffn1_w_up, ffn1_w_down, ln_mix_g, w_in, qn_a_g, kn_a_g, lam_q1, lam_k1, lam_q2, lam_k2, subln_a_g, qn_c_g, kn_c_g, rel_bias, w_up_a, w_up_b, w_up_c, w_o, ln_ffn2_g, ffn2_w_gate, ffn2_w_up, ffn2_w_down):
    w = dict(ln_ffn1_g=ln_ffn1_g, ffn1_w_gate=ffn1_w_gate, ffn1_w_up=ffn1_w_up, ffn1_w_down=ffn1_w_down,
             ln_mix_g=ln_mix_g, w_in=w_in, qn_a_g=qn_a_g, kn_a_g=kn_a_g, lam_q1=lam_q1, lam_k1=lam_k1,
             lam_q2=lam_q2, lam_k2=lam_k2, subln_a_g=subln_a_g, qn_c_g=qn_c_g, kn_c_g=kn_c_g,
             w_up_a=w_up_a, w_up_b=w_up_b, w_up_c=w_up_c, w_o=w_o, ln_ffn2_g=ln_ffn2_g,
             ffn2_w_gate=ffn2_w_gate, ffn2_w_up=ffn2_w_up, ffn2_w_down=ffn2_w_down)
    depth = w_in.shape[0]
    assert x_prompt.shape[0] == 1 and x_sample.shape[1] == 1
    t = x_prompt.shape[1]
    bsz, n_pages = page_table.shape
    page = cache_a_k.shape[2]
    n_pool = cache_a_k.shape[1]
    past = n_pages * page
    assert t % MOBA_BLOCK == 0 and past % MOBA_BLOCK == 0 and t % RET_CHUNK == 0

    bd = _block_diag_ones()
    cos_p, sin_p = _rotary_tables(jnp.arange(t, dtype=jnp.int32))
    cos_s, sin_s = _rotary_tables(jnp.full((bsz,), past, jnp.int32))
    ret_consts = _ret_constants(RET_CHUNK)
    gammas = ret_consts[4]

    taq, tak = min(ATT_A_TQ, t), min(ATT_A_TK, t)
    sched_a = _tile_schedule(t, taq, tak)
    bias_a = _bias_tiles(rel_bias, sched_a[4], H_A, 0, taq, tak, 1, LOG2E)
    tcq = min(ATT_C_TQ, t)
    sched_c = _tile_schedule(t, tcq, MOBA_BLOCK)
    bias_c = _bias_tiles(rel_bias, sched_c[4], H_C, H_A, tcq, MOBA_BLOCK, 1, LOG2E)

    npg = min(PAGES_PER_STEP, n_pages)
    dec_steps = n_pages // npg
    assert npg * page >= T5_FAR
    bias_da = _bias_tiles(rel_bias, [1 << 24, past - (dec_steps - 1) * npg * page], H_A, 0, 8, npg * page, 0)
    bias_da = jnp.repeat(bias_da[:, :, 0, :], 2, axis=1)
    same_head = (jnp.arange(2 * H_A) // 2)[:, None, None] == jnp.arange(H_A)[None, None, :]
    bias_da = jnp.where(same_head[None], bias_da[..., None], NEG).reshape(2, 2 * H_A, npg * page * H_A)
    bias_da = jnp.concatenate([bias_da, jnp.zeros_like(bias_da)], axis=1)
    n_past_blk = past // MOBA_BLOCK
    bias_dc = _bias_tiles(rel_bias, [past - j * MOBA_BLOCK for j in range(n_past_blk)], H_C, H_A, 8, MOBA_BLOCK, 0)
    bias_dc = bias_dc[:, :, 0:1, :]
    b_self = rel_bias[0] - rel_bias[N_BUCKETS - 1]
    b0_a = jnp.concatenate([jnp.repeat(b_self[:H_A], 2), jnp.zeros((8,), F32)])[:, None]
    b0_c = jnp.broadcast_to(b_self[H_A:, None, None], (H_C, 1, HEAD_DIM))

    ck_a = cache_a_k.reshape(depth, n_pool, page * H_A, 2 * HEAD_DIM)
    cv_a = cache_a_v.reshape(depth, n_pool, page * H_A, 2 * HEAD_DIM)
    ck_c = jnp.transpose(cache_c_k, (0, 1, 3, 4, 2)).reshape(depth, n_pool, W_C, page)
    cv_c = jnp.transpose(cache_c_v, (0, 1, 3, 4, 2)).reshape(depth, n_pool, W_C, page)
    comp_mask = jnp.eye(2, dtype=F32)[None, None, :, :, None]

    xp = x_prompt[0]
    xs = x_sample[:, 0]
    rows_p, rows_s = [], []
    for l in range(depth):
        lw = _layer_weights(l, w)
        lam_init = 0.8 - 0.6 * math.exp(-0.3 * l)

        xp = _ffn(xp, *lw["ffn1"])
        qa_bf, ka, ka_bf, va, va_bf = _proj(xp, lw["ln_mix"], lw["w_a"], SEG_A_PROMPT, gains=lw["gains_a"], bd=bd)
        qb, kb, vb, gb = _proj(xp, lw["ln_mix"], lw["w_b"], SEG_B, cos=cos_p, sin=sin_p)
        qc, kc, kc_bf, vc, vc_bf = _proj(xp, lw["ln_mix"], lw["w_c"], SEG_C_PROMPT, gains=lw["gains_c"], bd=bd)
        oa = _attn_a(qa_bf, ka_bf, va_bf, bias_a, sched_a, lw["lam_vecs"], lw["subln"], lam_init)
        s0 = jnp.zeros((H_B // 2, 2 * DK_B, DV_B), F32)
        ob, s_fin = _retention(qb, kb, vb, gb, s0, ret_consts)
        kmean = _kmean(kc)
        kmean_pad = jnp.pad(kmean, ((0, LANES - kmean.shape[0]), (0, 0)))
        qc_aug = _select(qc, kmean_pad, 0)
        oc = _attn_c(qc_aug, kc_bf, vc_bf, bias_c, sched_c)
        xp = _merge(xp, lw["ln_mix"], lw["w_g"], oa, ob, oc, lw["w_up_a"], lw["w_up_b"], lw["w_up_c"], lw["w_o"])
        xp = _ffn(xp, *lw["ffn2"])
        rows_p.append((ka.reshape(1, t, H_A, 2 * HEAD_DIM), va.reshape(1, t, H_A, 2 * HEAD_DIM),
                       kc.reshape(1, t, H_C, HEAD_DIM), vc.reshape(1, t, H_C, HEAD_DIM),
                       s_fin.reshape(1, H_B, DK_B, DV_B)))

        xs = _ffn(xs, *lw["ffn1"])
        sqa_bf, ska, _, sva, _ = _proj(xs, lw["ln_mix"], lw["w_a"], SEG_A, gains=lw["gains_a"], bd=bd)
        sqb, skb, svb, sgb = _proj(xs, lw["ln_mix"], lw["w_b"], SEG_B, cos=cos_s, sin=sin_s)
        sqc, sqc_bf, skc, _, svc, _ = _proj(xs, lw["ln_mix"], lw["w_c"], SEG_C, gains=lw["gains_c"], bd=bd)
        pad_rows = lambda a: jnp.pad(a, ((0, 0), (0, DEC_ROWS - 2 * H_A), (0, 0)))
        q_rows = (sqa_bf.astype(F32).reshape(bsz, H_A, 1, 2, HEAD_DIM) * comp_mask).reshape(bsz, 2 * H_A, 2 * HEAD_DIM)
        per_head = lambda a: jnp.repeat(a.reshape(bsz, H_A, 2 * HEAD_DIM), 2, axis=1)
        soa = _dec_a(l, page_table, pad_rows(q_rows), ck_a, cv_a, bias_da, pad_rows(per_head(ska)),
                     pad_rows(per_head(sva)), b0_a, lw["lam_vecs"], lw["subln"], lam_init)
        sob, s_new = _dec_ret(l, sqb, skb, svb, sgb, state_ret, gammas)
        idx = _dec_c1(l, page_table, sqc.reshape(bsz, W_C, 1), ck_c)
        h4 = lambda a: a.reshape(bsz, H_C, 1, HEAD_DIM)
        soc = _dec_c2(l, page_table, idx, h4(sqc_bf.astype(F32)), ck_c, cv_c, bias_dc, h4(skc), h4(svc), b0_c)
        xs = _merge(xs, lw["ln_mix"], lw["w_g"], soa.reshape(bsz, W_A).astype(BF16),
                    sob.reshape(bsz, W_BV).astype(BF16), soc.reshape(bsz, W_C).astype(BF16),
                    lw["w_up_a"], lw["w_up_b"], lw["w_up_c"], lw["w_o"])
        xs = _ffn(xs, *lw["ffn2"])
        rows_s.append((ska.reshape(bsz, 1, H_A, 2 * HEAD_DIM), sva.reshape(bsz, 1, H_A, 2 * HEAD_DIM),
                       skc.reshape(bsz, 1, H_C, HEAD_DIM), svc.reshape(bsz, 1, H_C, HEAD_DIM), s_new))

    stack = lambda rows, i: jnp.stack([r[i] for r in rows])
    return (xp[None], xs[:, None, :],
            stack(rows_p, 0), stack(rows_p, 1), stack(rows_p, 2), stack(rows_p, 3), stack(rows_p, 4),
            stack(rows_s, 0), stack(rows_s, 1), stack(rows_s, 2), stack(rows_s, 3), stack(rows_s, 4))
```
